```python
import functools
import jax, jax.numpy as jnp
from jax import lax
import numpy as np

D_MODEL = 4096
BATCH = 1
SEQ = 8192
DEPTH = 1
DEC_BATCH = 128
DEC_SEQ = 8
PAST_LEN = 2048
PAGE_SIZE = 128

MIX_WIDTH = D_MODEL
ATTN_WIDTH = MIX_WIDTH // 2
CONV_WIDTH = MIX_WIDTH - ATTN_WIDTH
HEAD_DIM = 128
N_HEADS = ATTN_WIDTH // HEAD_DIM
N_KV_HEADS = 4
GROUP = N_HEADS // N_KV_HEADS
KV_WIDTH = N_KV_HEADS * HEAD_DIM
N_IDX_HEADS = 32
IDX_DIM = 128
INDEX_TOPK = 256
CONV_K = 31
D_FF = 4 * D_MODEL
Q_BLOCK = 128
N_MOD = 6
EPS = 1e-6

IN_SIZES = (ATTN_WIDTH, KV_WIDTH, KV_WIDTH, N_IDX_HEADS * IDX_DIM, IDX_DIM, N_IDX_HEADS, 2 * CONV_WIDTH)
D_IN = sum(IN_SIZES)
IN_SPLITS = tuple(int(s) for s in np.cumsum(IN_SIZES)[:-1])

kernel_name = "hymba_dsa_conformer_conv_decoder_step"


def rms_norm(x, g):
    xf = x.astype(jnp.float32)
    y = xf * lax.rsqrt(jnp.mean(xf * xf, axis=-1, keepdims=True) + EPS)
    return (y * g.astype(jnp.float32)).astype(x.dtype)


def modulate(h, shift, scale):
    return h * (1 + scale[:, None, :]) + shift[:, None, :]


def ada_split(c, w, b, n):
    m = jax.nn.silu(c) @ w + b
    return jnp.split(m, n, axis=-1)


def alibi_slopes():
    return 2.0 ** (-8.0 * jnp.arange(1, N_HEADS + 1, dtype=jnp.float32) / N_HEADS)


def dsa_attend(q, iq, iw, k, v, ik, q_pos, k_pos, topk):
    f32 = jnp.float32
    tq = q.shape[0]
    causal = k_pos[None, :] <= q_pos[:, None]
    dots = jnp.einsum('thd,sd->ths', iq, ik, preferred_element_type=f32) * IDX_DIM ** -0.5
    score = jnp.einsum('th,ths->ts', iw.astype(f32), jax.nn.relu(dots))
    score = jnp.where(causal, score, -jnp.inf)
    _, sel = lax.top_k(score, topk)
    sel_pos = k_pos[sel]
    valid = sel_pos <= q_pos[:, None]
    k_sel = k[sel]
    v_sel = v[sel]
    qg = q.reshape(tq, N_KV_HEADS, GROUP, HEAD_DIM)
    logits = jnp.einsum('tkgd,tskd->tkgs', qg, k_sel, preferred_element_type=f32) * HEAD_DIM ** -0.5
    dist = (q_pos[:, None] - sel_pos).astype(f32)
    slopes = alibi_slopes().reshape(N_KV_HEADS, GROUP)
    logits = logits - slopes[None, :, :, None] * dist[:, None, None, :]
    logits = jnp.where(valid[:, None, None, :], logits, -jnp.inf)
    p = jax.nn.softmax(logits, axis=-1).astype(v.dtype)
    out = jnp.einsum('tkgs,tskd->tkgd', p, v_sel)
    return out.reshape(tq, ATTN_WIDTH)


def prompt_attend(q, k, v, iq, ik, iw):
    n, t = q.shape[:2]
    topk = min(INDEX_TOPK, t // 4)
    nb = t // Q_BLOCK
    pos = jnp.arange(t, dtype=jnp.int32)

    def blocks(a):
        return jnp.moveaxis(a.reshape((n, nb, Q_BLOCK) + a.shape[2:]), 1, 0)

    attend_seq = jax.vmap(functools.partial(dsa_attend, topk=topk),
                          in_axes=(0, 0, 0, 0, 0, 0, None, None))

    def one_block(args):
        qb, iqb, iwb, pb = args
        return attend_seq(qb, iqb, iwb, k, v, ik, pb, pos)

    out = lax.map(one_block, (blocks(q), blocks(iq), blocks(iw), pos.reshape(nb, Q_BLOCK)))
    return jnp.moveaxis(out, 0, 1).reshape(n, t, ATTN_WIDTH)


def make_sample_attend(k_pages, v_pages, ik_pages, page_table):
    def past(pages):
        g = pages[page_table]
        return g.reshape((g.shape[0], -1) + g.shape[3:])

    def attend(q, k, v, iq, ik, iw):
        t = q.shape[1]
        past_len = page_table.shape[1] * k_pages.shape[1]
        k_all = jnp.concatenate([past(k_pages).astype(k.dtype), k], axis=1)
        v_all = jnp.concatenate([past(v_pages).astype(v.dtype), v], axis=1)
        ik_all = jnp.concatenate([past(ik_pages).astype(ik.dtype), ik], axis=1)
        l_keys = past_len + t
        topk = min(INDEX_TOPK, l_keys // 4)
        k_pos = jnp.arange(l_keys, dtype=jnp.int32)
        q_pos = past_len + jnp.arange(t, dtype=jnp.int32)
        f = jax.vmap(functools.partial(dsa_attend, topk=topk),
                     in_axes=(0, 0, 0, 0, 0, 0, None, None))
        return f(q, iq, iw, k_all, v_all, ik_all, q_pos, k_pos)

    return attend


def conv_module(u, buf, w, b, ln_g, ln_b):
    a, g = jnp.split(u, 2, axis=-1)
    z = a * jax.nn.sigmoid(g)
    zp = jnp.concatenate([buf.astype(z.dtype), z], axis=1)
    y = lax.conv_general_dilated(zp, w[:, None, :].astype(z.dtype), window_strides=(1,),
                                 padding='VALID', dimension_numbers=('NWC', 'WIO', 'NWC'),
                                 feature_group_count=CONV_WIDTH) + b.astype(z.dtype)
    yf = y.astype(jnp.float32)
    mu = jnp.mean(yf, axis=-1, keepdims=True)
    var = jnp.mean(jnp.square(yf - mu), axis=-1, keepdims=True)
    yn = (yf - mu) * lax.rsqrt(var + EPS) * ln_g.astype(jnp.float32) + ln_b.astype(jnp.float32)
    return jax.nn.silu(yn).astype(u.dtype), zp[:, -(CONV_K - 1):]


def layer_forward(x, c, lp, attend, conv_buf):
    (w_ada, b_ada, g_mix, w_in, w_o, conv_w, conv_b, ln_g, ln_b, g_mlp, w_up, w_down) = lp
    sh1, sc1, gt1, sh2, sc2, gt2 = ada_split(c, w_ada, b_ada, N_MOD)
    n, t, _ = x.shape
    h = modulate(rms_norm(x, g_mix), sh1, sc1)
    q, k, v, iq, ik, iw, u = jnp.split(h @ w_in, IN_SPLITS, axis=-1)
    q = q.reshape(n, t, N_HEADS, HEAD_DIM)
    k = k.reshape(n, t, N_KV_HEADS, HEAD_DIM)
    v = v.reshape(n, t, N_KV_HEADS, HEAD_DIM)
    iq = iq.reshape(n, t, N_IDX_HEADS, IDX_DIM)
    iw = iw * N_IDX_HEADS ** -0.5
    attn = attend(q, k, v, iq, ik, iw)
    conv, conv_state = conv_module(u, conv_buf, conv_w, conv_b, ln_g, ln_b)
    x = x + gt1[:, None, :] * (jnp.concatenate([attn, conv], axis=-1) @ w_o)
    h2 = modulate(rms_norm(x, g_mlp), sh2, sc2)
    x = x + gt2[:, None, :] * (jnp.square(jax.nn.relu(h2 @ w_up)) @ w_down)
    return x, k, v, ik, conv_state


def setup_inputs(seed: int = 0) -> dict:
    key = jax.random.key(seed)
    ks = jax.random.split(key, 32)
    f32 = jnp.float32
    n_pages = PAST_LEN // PAGE_SIZE
    n_pool = (DEC_BATCH * n_pages * 5) // 4

    def nrm(k, shape, s):
        return jax.random.normal(k, shape, f32) * s

    page_table = jax.random.permutation(ks[6], n_pool)[:DEC_BATCH * n_pages]
    page_table = page_table.reshape(DEC_BATCH, n_pages).astype(jnp.int32)
    return {
        "x_prompt": nrm(ks[0], (BATCH, SEQ, D_MODEL), 1.0),
        "x_sample": nrm(ks[1], (DEC_BATCH, DEC_SEQ, D_MODEL), 1.0),
        "cache_k": nrm(ks[2], (DEPTH, n_pool, PAGE_SIZE, N_KV_HEADS, HEAD_DIM), 1.0),
        "cache_v": nrm(ks[3], (DEPTH, n_pool, PAGE_SIZE, N_KV_HEADS, HEAD_DIM), 1.0),
        "cache_idx_k": nrm(ks[4], (DEPTH, n_pool, PAGE_SIZE, IDX_DIM), 1.0),
        "state_conv": nrm(ks[5], (DEPTH, DEC_BATCH, CONV_K - 1, CONV_WIDTH), 0.5),
        "page_table": page_table,
        "c_prompt": nrm(ks[7], (BATCH, D_MODEL), 1.0),
        "c_sample": nrm(ks[8], (DEC_BATCH, D_MODEL), 1.0),
        "w_ada": nrm(ks[9], (DEPTH, D_MODEL, N_MOD * D_MODEL), 0.5 * D_MODEL ** -0.5),
        "b_ada": nrm(ks[10], (DEPTH, N_MOD * D_MODEL), 0.01),
        "g_mix": 1.0 + nrm(ks[11], (DEPTH, D_MODEL), 0.05),
        "w_in": nrm(ks[12], (DEPTH, D_MODEL, D_IN), D_MODEL ** -0.5),
        "w_o": nrm(ks[13], (DEPTH, MIX_WIDTH, D_MODEL), MIX_WIDTH ** -0.5),
        "conv_w": nrm(ks[14], (DEPTH, CONV_K, CONV_WIDTH), CONV_K ** -0.5),
        "conv_b": nrm(ks[15], (DEPTH, CONV_WIDTH), 0.01),
        "conv_ln_g": 1.0 + nrm(ks[16], (DEPTH, CONV_WIDTH), 0.05),
        "conv_ln_b": nrm(ks[17], (DEPTH, CONV_WIDTH), 0.01),
        "g_mlp": 1.0 + nrm(ks[18], (DEPTH, D_MODEL), 0.05),
        "w_up": nrm(ks[19], (DEPTH, D_MODEL, D_FF), D_MODEL ** -0.5),
        "w_down": nrm(ks[20], (DEPTH, D_FF, D_MODEL), D_FF ** -0.5),
        "w_ada_final": nrm(ks[21], (D_MODEL, 2 * D_MODEL), 0.5 * D_MODEL ** -0.5),
        "b_ada_final": nrm(ks[22], (2 * D_MODEL,), 0.01),
        "g_final": 1.0 + nrm(ks[23], (D_MODEL,), 0.05),
    }


def reference(x_prompt, x_sample, cache_k, cache_v, cache_idx_k, state_conv, page_table,
              c_prompt, c_sample, w_ada, b_ada, g_mix, w_in, w_o, conv_w, conv_b, conv_ln_g,
              conv_ln_b, g_mlp, w_up, w_down, w_ada_final, b_ada_final, g_final):
    xp, xs = x_prompt, x_sample
    kp_l, vp_l, ikp_l, cvp_l = [], [], [], []
    ks_l, vs_l, iks_l, cvs_l = [], [], [], []
    for l in range(DEPTH):
        lp = (w_ada[l], b_ada[l], g_mix[l], w_in[l], w_o[l], conv_w[l], conv_b[l],
              conv_ln_g[l], conv_ln_b[l], g_mlp[l], w_up[l], w_down[l])
        zero_buf = jnp.zeros((xp.shape[0], CONV_K - 1, CONV_WIDTH), xp.dtype)
        xp, kp, vp, ikp, cvp = layer_forward(xp, c_prompt, lp, prompt_attend, zero_buf)
        step_attend = make_sample_attend(cache_k[l], cache_v[l], cache_idx_k[l], page_table)
        xs, ks_, vs_, iks, cvs = layer_forward(xs, c_sample, lp, step_attend, state_conv[l])
        kp_l.append(kp); vp_l.append(vp); ikp_l.append(ikp); cvp_l.append(cvp)
        ks_l.append(ks_); vs_l.append(vs_); iks_l.append(iks); cvs_l.append(cvs)
    shp, scp = ada_split(c_prompt, w_ada_final, b_ada_final, 2)
    shs, scs = ada_split(c_sample, w_ada_final, b_ada_final, 2)
    y_prompt = modulate(rms_norm(xp, g_final), shp, scp)
    y_sample = modulate(rms_norm(xs, g_final), shs, scs)
    return (y_prompt, y_sample,
            jnp.stack(kp_l), jnp.stack(vp_l), jnp.stack(ikp_l), jnp.stack(cvp_l),
            jnp.stack(ks_l), jnp.stack(vs_l), jnp.stack(iks_l), jnp.stack(cvs_l))
```

```python
import functools

import jax
import jax.numpy as jnp
from jax import lax
from jax.experimental import pallas as pl
from jax.experimental.pallas import tpu as pltpu

F32 = jnp.float32
BF16 = jnp.bfloat16
I32 = jnp.int32

HEAD_DIM = 128
N_HEADS = 16
N_KV_HEADS = 4
GROUP = N_HEADS // N_KV_HEADS
N_IDX_HEADS = 32
IDX_DIM = 128
INDEX_TOPK = 256
CONV_K = 31
EPS = 1e-6
PAGE = 128

LANES = 128
INT_MIN = -2 ** 31
NEG_BIG = -1e30
VMEM_LIMIT = 56 * 1024 * 1024

SLOPES = tuple(2.0 ** (-8.0 * (h + 1) / N_HEADS) for h in range(N_HEADS))
IDX_SCALE = (N_IDX_HEADS ** -0.5) * (IDX_DIM ** -0.5)
ATT_SCALE = HEAD_DIM ** -0.5


def _cparams(sem):
    return pltpu.CompilerParams(dimension_semantics=sem, vmem_limit_bytes=VMEM_LIMIT)


def _ada_kernel(c_ref, w_ref, b_ref, o_ref):
    c = c_ref[...]
    a = (c * jax.nn.sigmoid(c)).astype(BF16)
    o_ref[...] = jnp.dot(a, w_ref[...].astype(BF16), preferred_element_type=F32) + b_ref[...]


def ada_matmul(c, w, b, tn=512):
    m, d = c.shape
    n = w.shape[1]
    return pl.pallas_call(
        _ada_kernel,
        grid=(n // tn,),
        in_specs=[pl.BlockSpec((m, d), lambda j: (0, 0)),
                  pl.BlockSpec((d, tn), lambda j: (0, j)),
                  pl.BlockSpec((1, tn), lambda j: (0, j))],
        out_specs=pl.BlockSpec((m, tn), lambda j: (0, j)),
        out_shape=jax.ShapeDtypeStruct((m, n), F32),
        compiler_params=_cparams(("arbitrary",)),
        name="ada_matmul",
    )(c, w, b.reshape(1, n))


def _mod_spec(mod, tm, d):
    if mod.shape[0] == 1:
        return pl.BlockSpec((1, d), lambda i: (0, 0))
    return pl.BlockSpec((tm, d), lambda i: (i, 0))


def _norm_mod_kernel(x_ref, g_ref, sh_ref, sc_ref, o_ref):
    x = x_ref[...]
    y = x * lax.rsqrt(jnp.mean(x * x, axis=-1, keepdims=True) + EPS) * g_ref[...]
    o_ref[...] = (y * (1.0 + sc_ref[...]) + sh_ref[...]).astype(o_ref.dtype)


def norm_modulate(x, g, shift, scale, out_dtype, tm=256):
    m, d = x.shape
    tm = min(tm, m)
    return pl.pallas_call(
        _norm_mod_kernel,
        grid=(m // tm,),
        in_specs=[pl.BlockSpec((tm, d), lambda i: (i, 0)),
                  pl.BlockSpec((1, d), lambda i: (0, 0)),
                  _mod_spec(shift, tm, d), _mod_spec(scale, tm, d)],
        out_specs=pl.BlockSpec((tm, d), lambda i: (i, 0)),
        out_shape=jax.ShapeDtypeStruct((m, d), out_dtype),
        compiler_params=_cparams(("parallel",)),
        name="norm_modulate",
    )(x, g.reshape(1, d), shift, scale)


def _mm_kernel(a_ref, b_ref, o_ref):
    o_ref[...] = jnp.dot(a_ref[...], b_ref[...], preferred_element_type=F32).astype(o_ref.dtype)


def matmul(a, b, tm=1024, tn=768):
    m, k = a.shape
    n = b.shape[1]
    tm = min(tm, m)
    return pl.pallas_call(
        _mm_kernel,
        grid=(m // tm, n // tn),
        in_specs=[pl.BlockSpec((tm, k), lambda i, j: (i, 0)),
                  pl.BlockSpec((k, tn), lambda i, j: (0, j))],
        out_specs=pl.BlockSpec((tm, tn), lambda i, j: (i, j)),
        out_shape=jax.ShapeDtypeStruct((m, n), F32),
        compiler_params=_cparams(("parallel", "arbitrary")),
        name="in_proj",
    )(a, b)


def _wo_kernel(a1_ref, a2_ref, w1_ref, w2_ref, x_ref, gt_ref, o_ref):
    acc = jnp.dot(a1_ref[...].astype(BF16), w1_ref[...], preferred_element_type=F32)
    acc += jnp.dot(a2_ref[...].astype(BF16), w2_ref[...], preferred_element_type=F32)
    o_ref[...] = x_ref[...] + gt_ref[...] * acc


def out_proj_residual(attn, conv, w_o, x, gate, tm=1024, tn=512):
    m, ka = attn.shape
    kc = conv.shape[1]
    assert ka == kc
    n = w_o.shape[1]
    tm = min(tm, m)
    if gate.shape[0] == 1:
        gspec = pl.BlockSpec((1, tn), lambda i, j: (0, j))
    else:
        gspec = pl.BlockSpec((tm, tn), lambda i, j: (i, j))
    return pl.pallas_call(
        _wo_kernel,
        grid=(m // tm, n // tn),
        in_specs=[pl.BlockSpec((tm, ka), lambda i, j: (i, 0)),
                  pl.BlockSpec((tm, kc), lambda i, j: (i, 0)),
                  pl.BlockSpec((ka, tn), lambda i, j: (0, j)),
                  pl.BlockSpec((kc, tn), lambda i, j: (1, j)),
                  pl.BlockSpec((tm, tn), lambda i, j: (i, j)),
                  gspec],
        out_specs=pl.BlockSpec((tm, tn), lambda i, j: (i, j)),
        out_shape=jax.ShapeDtypeStruct((m, n), F32),
        compiler_params=_cparams(("parallel", "arbitrary")),
        name="out_proj",
    )(attn, conv, w_o, w_o, x, gate)


def _mlp_kernel(h_ref, wu_ref, wd_ref, o_ref):
    f = pl.program_id(1)
    up = jnp.dot(h_ref[...], wu_ref[...], preferred_element_type=F32)
    act = jnp.square(jnp.maximum(up, 0.0)).astype(BF16)
    d = jnp.dot(act, wd_ref[...], preferred_element_type=F32)

    @pl.when(f == 0)
    def _():
        o_ref[...] = d

    @pl.when(f > 0)
    def _():
        o_ref[...] += d


def mlp(h, w_up, w_down, tm=512, tf=512):
    m, d = h.shape
    ff = w_up.shape[1]
    tm = min(tm, m)
    return pl.pallas_call(
        _mlp_kernel,
        grid=(m // tm, ff // tf),
        in_specs=[pl.BlockSpec((tm, d), lambda i, f: (i, 0)),
                  pl.BlockSpec((d, tf), lambda i, f: (0, f)),
                  pl.BlockSpec((tf, d), lambda i, f: (f, 0))],
        out_specs=pl.BlockSpec((tm, d), lambda i, f: (i, 0)),
        out_shape=jax.ShapeDtypeStruct((m, d), F32),
        compiler_params=_cparams(("parallel", "arbitrary")),
        name="mlp",
    )(h, w_up, w_down)


def _final_kernel(x_ref, m_ref, gt_ref, g_ref, sh_ref, sc_ref, o_ref):
    x = x_ref[...] + gt_ref[...] * m_ref[...]
    y = x * lax.rsqrt(jnp.mean(x * x, axis=-1, keepdims=True) + EPS) * g_ref[...]
    o_ref[...] = y * (1.0 + sc_ref[...]) + sh_ref[...]


def residual_final_norm(x, mlp_out, gate, g, shift, scale, tm=256):
    m, d = x.shape
    tm = min(tm, m)
    return pl.pallas_call(
        _final_kernel,
        grid=(m // tm,),
        in_specs=[pl.BlockSpec((tm, d), lambda i: (i, 0)),
                  pl.BlockSpec((tm, d), lambda i: (i, 0)),
                  _mod_spec(gate, tm, d),
                  pl.BlockSpec((1, d), lambda i: (0, 0)),
                  _mod_spec(shift, tm, d), _mod_spec(scale, tm, d)],
        out_specs=pl.BlockSpec((tm, d), lambda i: (i, 0)),
        out_shape=jax.ShapeDtypeStruct((m, d), F32),
        compiler_params=_cparams(("parallel",)),
        name="final_norm",
    )(x, mlp_out, gate, g.reshape(1, d), shift, scale)


def _ln_silu(y, lg, lb):
    mu = jnp.mean(y, axis=-1, keepdims=True)
    yc = y - mu
    var = jnp.mean(yc * yc, axis=-1, keepdims=True)
    yn = yc * lax.rsqrt(var + EPS) * lg + lb
    return yn * jax.nn.sigmoid(yn)


CONV_HALO = 32
CONV_RT = 64
CONV_CT = 512


def _conv_prompt_kernel(u_ref, w_ref, b_ref, lg_ref, lb_ref, o_ref, st_ref, zbuf, ybuf):
    tt = u_ref.shape[0]
    c = w_ref.shape[1]
    i = pl.program_id(0)

    @pl.when(i == 0)
    def _():
        zbuf[0:CONV_HALO, :] = jnp.zeros((CONV_HALO, c), F32)

    zbuf[CONV_HALO:CONV_HALO + tt, :] = u_ref[:, 0:c] * jax.nn.sigmoid(u_ref[:, c:2 * c])
    base = CONV_HALO - (CONV_K - 1)
    for rt in range(tt // CONV_RT):
        for ct in range(c // CONV_CT):
            cs = slice(ct * CONV_CT, (ct + 1) * CONV_CT)
            acc = jnp.zeros((CONV_RT, CONV_CT), F32)
            for j in range(CONV_K):
                r0 = base + j + rt * CONV_RT
                acc = acc + w_ref[j:j + 1, cs] * zbuf[r0:r0 + CONV_RT, cs]
            ybuf[rt * CONV_RT:(rt + 1) * CONV_RT, cs] = acc + b_ref[:, cs]
    o_ref[...] = _ln_silu(ybuf[...], lg_ref[...], lb_ref[...]).astype(o_ref.dtype)
    st_ref[...] = zbuf[CONV_HALO + tt - (CONV_K - 1):CONV_HALO + tt, :]
    zbuf[0:CONV_HALO, :] = zbuf[tt:tt + CONV_HALO, :]


def conv_prompt(proj, conv_w, conv_b, ln_g, ln_b, tt=256):
    t = proj.shape[0]
    c = conv_w.shape[1]
    return pl.pallas_call(
        _conv_prompt_kernel,
        grid=(t // tt,),
        in_specs=[pl.BlockSpec((tt, 2 * c), lambda i: (i, 0)),
                  pl.BlockSpec((CONV_K, c), lambda i: (0, 0)),
                  pl.BlockSpec((1, c), lambda i: (0, 0)),
                  pl.BlockSpec((1, c), lambda i: (0, 0)),
                  pl.BlockSpec((1, c), lambda i: (0, 0))],
        out_specs=[pl.BlockSpec((tt, c), lambda i: (i, 0)),
                   pl.BlockSpec((CONV_K - 1, c), lambda i: (0, 0))],
        out_shape=[jax.ShapeDtypeStruct((t, c), BF16),
                   jax.ShapeDtypeStruct((CONV_K - 1, c), F32)],
        scratch_shapes=[pltpu.VMEM((CONV_HALO + tt, c), F32), pltpu.VMEM((tt, c), F32)],
        compiler_params=_cparams(("arbitrary",)),
        name="conv_prompt",
    )(proj, conv_w, conv_b.reshape(1, c), ln_g.reshape(1, c), ln_b.reshape(1, c))


def _conv_sample_kernel(u_ref, s_ref, w_ref, b_ref, lg_ref, lb_ref, o_ref, ns_ref, zp):
    bb, t, _ = u_ref.shape
    c = w_ref.shape[1]
    k1 = CONV_K - 1
    for b in range(bb):
        zp[0:k1, :] = s_ref[b]
        zp[k1:k1 + t, :] = u_ref[b, :, 0:c] * jax.nn.sigmoid(u_ref[b, :, c:2 * c])
        acc = jnp.zeros((t, c), F32)
        for j in range(CONV_K):
            acc = acc + w_ref[j:j + 1, :] * zp[j:j + t, :]
        o_ref[b] = _ln_silu(acc + b_ref[...], lg_ref[...], lb_ref[...])
        ns_ref[b] = zp[t:t + k1, :]


def conv_sample(proj3, state, conv_w, conv_b, ln_g, ln_b, bb=8):
    n, t, _ = proj3.shape
    c = conv_w.shape[1]
    k1 = CONV_K - 1
    return pl.pallas_call(
        _conv_sample_kernel,
        grid=(n // bb,),
        in_specs=[pl.BlockSpec((bb, t, 2 * c), lambda i: (i, 0, 0)),
                  pl.BlockSpec((bb, k1, c), lambda i: (i, 0, 0)),
                  pl.BlockSpec((CONV_K, c), lambda i: (0, 0)),
                  pl.BlockSpec((1, c), lambda i: (0, 0)),
                  pl.BlockSpec((1, c), lambda i: (0, 0)),
                  pl.BlockSpec((1, c), lambda i: (0, 0))],
        out_specs=[pl.BlockSpec((bb, t, c), lambda i: (i, 0, 0)),
                   pl.BlockSpec((bb, k1, c), lambda i: (i, 0, 0))],
        out_shape=[jax.ShapeDtypeStruct((n, t, c), F32),
                   jax.ShapeDtypeStruct((n, k1, c), F32)],
        scratch_shapes=[pltpu.VMEM((k1 + t + 2, c), F32)],
        compiler_params=_cparams(("parallel",)),
        name="conv_sample",
    )(proj3, state, conv_w, conv_b.reshape(1, c), ln_g.reshape(1, c), ln_b.reshape(1, c))


def _score_key(score, valid):
    bits = pltpu.bitcast(score, I32)
    key = bits ^ ((bits >> 31) & 0x7FFFFFFF)
    key = jnp.maximum(key, INT_MIN + 1)
    return jnp.where(valid, key, INT_MIN)


def _kth_largest_key(count_ge, rows):
    def bit_body(b, thr):
        cand = thr + jnp.left_shift(jnp.int32(1), 31 - b)
        return jnp.where(count_ge(cand) >= INDEX_TOPK, cand, thr)

    thr = lax.fori_loop(0, 32, bit_body, jnp.full((rows, LANES), INT_MIN, I32))
    return jnp.maximum(thr, INT_MIN + 1)


IDX_TR = 32
IDX_TN = 256


def _attn_prompt_kernel(iq_ref, q_ref, iw_ref, ikt_ref, kt_ref, v_ref, o_ref,
                        iq_r, wb, keys, q_r, acc, m_s, maskb):
    tq = q_ref.shape[0]
    ts = ikt_ref.shape[2]
    qb = pl.program_id(0)
    q0 = qb * tq
    nj = (q0 + tq - 1) // ts + 1
    n_sub = tq // IDX_TR

    for s in range(n_sub):
        rs = slice(s * IDX_TR, (s + 1) * IDX_TR)
        for h in range(N_IDX_HEADS):
            hs = slice(h * IDX_TR, (h + 1) * IDX_TR)
            iq_r[s, hs, :] = iq_ref[rs, h * IDX_DIM:(h + 1) * IDX_DIM].astype(BF16)
            wb[s, hs, :] = jnp.broadcast_to(iw_ref[rs, h:h + 1] * IDX_SCALE, (IDX_TR, LANES))

    def score_chunk(j, carry):
        for s in range(n_sub):
            t_pos = q0 + s * IDX_TR + lax.broadcasted_iota(I32, (IDX_TR, LANES), 0)
            for cn in range(ts // IDX_TN):
                d = jnp.dot(iq_r[s], ikt_ref[j, :, cn * IDX_TN:(cn + 1) * IDX_TN],
                            preferred_element_type=F32)
                for half in range(IDX_TN // LANES):
                    ls = slice(half * LANES, (half + 1) * LANES)
                    sc = jnp.zeros((IDX_TR, LANES), F32)
                    for h in range(N_IDX_HEADS):
                        hs = slice(h * IDX_TR, (h + 1) * IDX_TR)
                        sc = sc + wb[s, hs, :] * jnp.maximum(d[hs, ls], 0.0)
                    c0 = cn * IDX_TN + half * LANES
                    s_pos = j * ts + c0 + lax.broadcasted_iota(I32, (IDX_TR, LANES), 1)
                    keys[j, s * IDX_TR:(s + 1) * IDX_TR, c0:c0 + LANES] = _score_key(sc, s_pos <= t_pos)
        return carry

    lax.fori_loop(0, nj, score_chunk, 0)

    def count_ge(cand):
        def body(j, cnt):
            for cn in range(ts // LANES):
                cnt = cnt + (keys[j, :, cn * LANES:(cn + 1) * LANES] >= cand).astype(I32)
            return cnt
        cnt = lax.fori_loop(0, nj, body, jnp.zeros((tq, LANES), I32))
        return jnp.sum(cnt, axis=1, keepdims=True)

    thr = _kth_largest_key(count_ge, tq)

    for g in range(N_KV_HEADS):
        for hh in range(GROUP):
            h = g * GROUP + hh
            q_r[g, hh * tq:(hh + 1) * tq, :] = q_ref[:, h * HEAD_DIM:(h + 1) * HEAD_DIM].astype(BF16)
    acc[...] = jnp.zeros(acc.shape, F32)
    m_s[...] = jnp.full(m_s.shape, -jnp.inf, F32)
    ones = jnp.ones((ts, HEAD_DIM), BF16)
    n_lt = ts // LANES

    def attend_chunk(j, carry):
        for cn in range(n_lt):
            ls = slice(cn * LANES, (cn + 1) * LANES)
            maskb[:, ls] = jnp.where(keys[j, :, ls] >= thr, 0.0, NEG_BIG)
        rel = (j * ts - q0 + lax.broadcasted_iota(I32, (1, ts), 1)).astype(F32)
        for g in range(N_KV_HEADS):
            s_all = jnp.dot(q_r[g], kt_ref[j, g * HEAD_DIM:(g + 1) * HEAD_DIM, :],
                            preferred_element_type=F32)
            v_aug = jnp.concatenate([v_ref[j, :, g * HEAD_DIM:(g + 1) * HEAD_DIM], ones], axis=1)
            for hh in range(GROUP):
                h = g * GROUP + hh
                lg = s_all[hh * tq:(hh + 1) * tq] * ATT_SCALE + (SLOPES[h] * rel) + maskb[...]
                m_old = m_s[h]
                m_new = jnp.maximum(m_old, jnp.max(lg, axis=1, keepdims=True))
                alpha = jnp.exp(m_old - m_new)
                p = jnp.exp(lg - jnp.concatenate([m_new] * n_lt, axis=1)).astype(BF16)
                pv = jnp.dot(p, v_aug, preferred_element_type=F32)
                acc[h] = acc[h] * jnp.concatenate([alpha, alpha], axis=1) + pv
                m_s[h] = m_new
        return carry

    lax.fori_loop(0, nj, attend_chunk, 0)

    for h in range(N_HEADS):
        a = acc[h]
        o_ref[:, h * HEAD_DIM:(h + 1) * HEAD_DIM] = (a[:, :HEAD_DIM] / a[:, HEAD_DIM:]).astype(o_ref.dtype)


def attn_prompt(proj, col_iq, col_q, col_iw, ikt3, kt3, v3, tq=128):
    t = proj.shape[0]
    nc, _, ts = ikt3.shape
    kvw = kt3.shape[1]
    aw = N_HEADS * HEAD_DIM
    iqw = N_IDX_HEADS * IDX_DIM
    resident = dict(pipeline_mode=pl.Buffered(1))
    return pl.pallas_call(
        _attn_prompt_kernel,
        grid=(t // tq,),
        in_specs=[pl.BlockSpec((tq, iqw), lambda i: (i, col_iq)),
                  pl.BlockSpec((tq, aw), lambda i: (i, col_q)),
                  pl.BlockSpec((tq, LANES), lambda i: (i, col_iw)),
                  pl.BlockSpec((nc, IDX_DIM, ts), lambda i: (0, 0, 0), **resident),
                  pl.BlockSpec((nc, kvw, ts), lambda i: (0, 0, 0), **resident),
                  pl.BlockSpec((nc, ts, kvw), lambda i: (0, 0, 0), **resident)],
        out_specs=pl.BlockSpec((tq, aw), lambda i: (i, 0)),
        out_shape=jax.ShapeDtypeStruct((t, aw), BF16),
        scratch_shapes=[pltpu.VMEM((tq // IDX_TR, N_IDX_HEADS * IDX_TR, IDX_DIM), BF16),
                        pltpu.VMEM((tq // IDX_TR, N_IDX_HEADS * IDX_TR, LANES), F32),
                        pltpu.VMEM((nc, tq, ts), I32),
                        pltpu.VMEM((N_KV_HEADS, GROUP * tq, HEAD_DIM), BF16),
                        pltpu.VMEM((N_HEADS, tq, 2 * HEAD_DIM), F32),
                        pltpu.VMEM((N_HEADS, tq, LANES), F32),
                        pltpu.VMEM((tq, ts), F32)],
        compiler_params=_cparams(("parallel",)),
        name="attn_prompt",
    )(proj, proj, proj, ikt3, kt3, v3)


def _attn_sample_kernel(pt_ref, iq_ref, q_ref, iw_ref, kn_ref, vn_ref, ikn_ref, kc_ref, vc_ref, ikc_ref,
                        o_ref, iq_r, wb, keys, k_s, v_s, ik_new, kv_new):
    t = q_ref.shape[1]
    n_pages = pl.num_programs(1)
    np_static = keys.shape[0] - 1
    p = pl.program_id(1)
    rows = N_IDX_HEADS * t

    @pl.when(p == 0)
    def _():
        for h in range(N_IDX_HEADS):
            iq_r[h * t:(h + 1) * t, :] = iq_ref[0, :, h * IDX_DIM:(h + 1) * IDX_DIM]
            wb[h * t:(h + 1) * t, :] = jnp.broadcast_to(iw_ref[0, :, h:h + 1] * IDX_SCALE, (t, LANES))

    def page_scores(ik_page):
        d = lax.dot_general(iq_r[...].astype(BF16), ik_page, (((1,), (1,)), ((), ())),
                            preferred_element_type=F32)
        sc = jnp.zeros((t, PAGE), F32)
        for h in range(N_IDX_HEADS):
            sc = sc + wb[h * t:(h + 1) * t, :] * jnp.maximum(d[h * t:(h + 1) * t, :], 0.0)
        return sc

    keys[p] = _score_key(page_scores(ikc_ref[0].astype(BF16)), jnp.full((t, PAGE), True))
    k_s[p] = kc_ref[0].astype(BF16)
    v_s[p] = vc_ref[0].astype(BF16)

    @pl.when(p == n_pages - 1)
    def _():
        ik_new[...] = jnp.zeros(ik_new.shape, F32)
        ik_new[0:t, :] = ikn_ref[0]
        kv_new[...] = jnp.zeros(kv_new.shape, F32)
        kv_new[0:t, :] = kn_ref[0]
        k_s[np_static] = kv_new[...].astype(BF16)
        kv_new[0:t, :] = vn_ref[0]
        v_s[np_static] = kv_new[...].astype(BF16)
        row = lax.broadcasted_iota(I32, (t, PAGE), 0)
        col = lax.broadcasted_iota(I32, (t, PAGE), 1)
        keys[np_static] = _score_key(page_scores(ik_new[...].astype(BF16)), col <= row)

        def count_ge(cand):
            cnt = jnp.zeros((t, LANES), I32)
            for pg in range(np_static + 1):
                cnt = cnt + (keys[pg] >= cand).astype(I32)
            return jnp.sum(cnt, axis=1, keepdims=True)

        thr = _kth_largest_key(count_ge, t)

        past = np_static * PAGE
        grow = lax.broadcasted_iota(I32, (GROUP * t, LANES), 0) // t
        lane = lax.broadcasted_iota(I32, (1, PAGE), 1)
        for g in range(N_KV_HEADS):
            gs = slice(g * HEAD_DIM, (g + 1) * HEAD_DIM)
            qg = jnp.concatenate([q_ref[0, :, (g * GROUP + hh) * HEAD_DIM:(g * GROUP + hh + 1) * HEAD_DIM]
                                  for hh in range(GROUP)], axis=0).astype(BF16)
            slope = jnp.full((GROUP * t, LANES), SLOPES[g * GROUP], F32)
            for hh in range(1, GROUP):
                slope = jnp.where(grow == hh, SLOPES[g * GROUP + hh], slope)
            logits = []
            for pg in range(np_static + 1):
                s = lax.dot_general(qg, k_s[pg, :, gs], (((1,), (1,)), ((), ())),
                                    preferred_element_type=F32)
                mb = jnp.where(keys[pg] >= thr, 0.0, NEG_BIG)
                rel = (pg * PAGE - past + lane).astype(F32)
                logits.append(s * ATT_SCALE + slope * rel + jnp.concatenate([mb] * GROUP, axis=0))
            mx = logits[0]
            for lg in logits[1:]:
                mx = jnp.maximum(mx, lg)
            mx = jnp.max(mx, axis=1, keepdims=True)
            num = jnp.zeros((GROUP * t, HEAD_DIM), F32)
            den = jnp.zeros((GROUP * t, LANES), F32)
            for pg in range(np_static + 1):
                pr = jnp.exp(logits[pg] - mx)
                den = den + pr
                num = num + jnp.dot(pr.astype(BF16), v_s[pg, :, gs], preferred_element_type=F32)
            out = num / jnp.sum(den, axis=1, keepdims=True)
            for hh in range(GROUP):
                h = g * GROUP + hh
                o_ref[0, :, h * HEAD_DIM:(h + 1) * HEAD_DIM] = out[hh * t:(hh + 1) * t, :]


def attn_sample(page_table, proj3, cols, cache_k, cache_v, cache_ik):
    n, t, _ = proj3.shape
    n_pages = page_table.shape[1]
    kvw = cache_k.shape[2]
    aw = N_HEADS * HEAD_DIM
    iqw = N_IDX_HEADS * IDX_DIM

    def pspec(width, col):
        return pl.BlockSpec((1, t, width), lambda b, p, pt: (b, 0, col))

    def cspec(width):
        return pl.BlockSpec((1, PAGE, width), lambda b, p, pt: (pt[b, p], 0, 0))

    grid_spec = pltpu.PrefetchScalarGridSpec(
        num_scalar_prefetch=1,
        grid=(n, n_pages),
        in_specs=[pspec(iqw, cols["iq"]), pspec(aw, cols["q"]), pspec(LANES, cols["iw"]),
                  pspec(kvw, cols["k"]), pspec(kvw, cols["v"]), pspec(IDX_DIM, cols["ik"]),
                  cspec(kvw), cspec(kvw), cspec(IDX_DIM)],
        out_specs=pl.BlockSpec((1, t, aw), lambda b, p, pt: (b, 0, 0)),
        scratch_shapes=[pltpu.VMEM((N_IDX_HEADS * t, IDX_DIM), F32),
                        pltpu.VMEM((N_IDX_HEADS * t, LANES), F32),
                        pltpu.VMEM((n_pages + 1, t, PAGE), I32),
                        pltpu.VMEM((n_pages + 1, PAGE, kvw), BF16),
                        pltpu.VMEM((n_pages + 1, PAGE, kvw), BF16),
                        pltpu.VMEM((PAGE, IDX_DIM), F32),
                        pltpu.VMEM((PAGE, kvw), F32)],
    )
    return pl.pallas_call(
        _attn_sample_kernel,
        grid_spec=grid_spec,
        out_shape=jax.ShapeDtypeStruct((n, t, aw), F32),
        compiler_params=_cparams(("parallel", "arbitrary")),
        name="attn_sample",
    )(page_table, proj3, proj3, proj3, proj3, proj3, proj3, cache_k, cache_v, cache_ik)


ATT_TS = 512


def _layer_weights(w_in, w_o, w_up, w_down, d_model):
    aw = N_HEADS * HEAD_DIM
    kvw = N_KV_HEADS * HEAD_DIM
    iqw = N_IDX_HEADS * IDX_DIM
    sizes = (aw, kvw, kvw, iqw, IDX_DIM, N_IDX_HEADS, w_in.shape[1] - (aw + 2 * kvw + iqw + IDX_DIM + N_IDX_HEADS))
    offs = [0]
    for s in sizes:
        offs.append(offs[-1] + s)
    wq, wk, wv, wiq, wik, wiw, wu = (w_in[:, offs[i]:offs[i + 1]] for i in range(7))
    wiw = jnp.pad(wiw, ((0, 0), (0, LANES - N_IDX_HEADS)))
    w_proj = jnp.concatenate([wu, wiq, wq, wk, wv, wik, wiw], axis=1).astype(BF16)
    cu = sizes[6]
    assert cu == iqw and cu % aw == 0 and (cu + iqw + aw) % kvw == 0
    cols = {"u": 0, "iq": cu // iqw, "q": (cu + iqw) // aw, "k": (cu + iqw + aw) // kvw,
            "v": (cu + iqw + aw) // kvw + 1, "ik": (cu + iqw + aw + 2 * kvw) // IDX_DIM,
            "iw": (cu + iqw + aw + 2 * kvw) // IDX_DIM + 1}
    return w_proj, cols, w_o.astype(BF16), w_up.astype(BF16), w_down.astype(BF16)


def kernel(x_prompt, x_sample, cache_k, cache_v, cache_idx_k, state_conv, page_table, c_prompt, c_sample, w_ada, b_ada, g_mix, w_in, w_o, conv_w, conv_b, conv_ln_g, conv_ln_b, g_mlp, w_up, w_down, w_ada_final, b_ada_final, g_final):
    nb, seq, d = x_prompt.shape
    ns, ts_, _ = x_sample.shape
    depth = w_ada.shape[0]
    assert nb == 1 and depth == 1
    kvw = N_KV_HEADS * HEAD_DIM
    cw = conv_w.shape[2]
    k1 = CONV_K - 1

    n_c = ns + nb
    pad = (-n_c) % 16
    c_all = jnp.concatenate([c_sample, c_prompt, jnp.zeros((pad, d), F32)], axis=0)
    mods = ada_matmul(c_all, w_ada[0], b_ada[0])
    mods_f = ada_matmul(c_all, w_ada_final, b_ada_final)

    def mod_p(arr, idx):
        return arr[ns:ns + 1, idx * d:(idx + 1) * d]

    def mod_s(arr, idx):
        return jnp.repeat(arr[:ns, idx * d:(idx + 1) * d], ts_, axis=0)

    w_proj, cols, w_o_b, w_up_b, w_down_b = _layer_weights(w_in[0], w_o[0], w_up[0], w_down[0], d)
    kcol = cols["k"] * kvw
    ikcol = cols["ik"] * IDX_DIM

    def trunk(x2, mod, attn, conv):
        x1 = out_proj_residual(attn, conv, w_o_b, x2, mod(mods, 2))
        h2 = norm_modulate(x1, g_mlp[0], mod(mods, 3), mod(mods, 4), BF16)
        m = mlp(h2, w_up_b, w_down_b)
        return residual_final_norm(x1, m, mod(mods, 5), g_final, mod(mods_f, 0), mod(mods_f, 1))

    xp = x_prompt.reshape(seq, d)
    hp = norm_modulate(xp, g_mix[0], mod_p(mods, 0), mod_p(mods, 1), BF16)
    proj_p = matmul(hp, w_proj)
    k_p = proj_p[:, kcol:kcol + kvw]
    v_p = proj_p[:, kcol + kvw:kcol + 2 * kvw]
    ik_p = proj_p[:, ikcol:ikcol + IDX_DIM]
    nc = seq // ATT_TS
    ikt3 = ik_p.astype(BF16).reshape(nc, ATT_TS, IDX_DIM).transpose(0, 2, 1)
    kt3 = k_p.astype(BF16).reshape(nc, ATT_TS, kvw).transpose(0, 2, 1)
    v3 = v_p.astype(BF16).reshape(nc, ATT_TS, kvw)
    attn_p = attn_prompt(proj_p, cols["iq"], cols["q"], cols["iw"], ikt3, kt3, v3)
    conv_p, cstate_p = conv_prompt(proj_p, conv_w[0], conv_b[0], conv_ln_g[0], conv_ln_b[0])
    y_p = trunk(xp, mod_p, attn_p, conv_p)

    xs = x_sample.reshape(ns * ts_, d)
    hs = norm_modulate(xs, g_mix[0], mod_s(mods, 0), mod_s(mods, 1), BF16)
    proj_s = matmul(hs, w_proj)
    proj_s3 = proj_s.reshape(ns, ts_, proj_s.shape[1])
    n_pool = cache_k.shape[1]
    attn_s = attn_sample(page_table, proj_s3, cols,
                         cache_k[0].reshape(n_pool, PAGE, kvw), cache_v[0].reshape(n_pool, PAGE, kvw),
                         cache_idx_k[0])
    conv_s, cstate_s = conv_sample(proj_s3, state_conv[0], conv_w[0], conv_b[0], conv_ln_g[0], conv_ln_b[0])
    y_s = trunk(xs, mod_s, attn_s.reshape(ns * ts_, -1).astype(BF16), conv_s.reshape(ns * ts_, cw).astype(BF16))

    return (y_p.reshape(nb, seq, d),
            y_s.reshape(ns, ts_, d),
            k_p.reshape(depth, nb, seq, N_KV_HEADS, HEAD_DIM),
            v_p.reshape(depth, nb, seq, N_KV_HEADS, HEAD_DIM),
            ik_p.reshape(depth, nb, seq, IDX_DIM),
            cstate_p.reshape(depth, nb, k1, cw),
            proj_s[:, kcol:kcol + kvw].reshape(depth, ns, ts_, N_KV_HEADS, HEAD_DIM),
            proj_s[:, kcol + kvw:kcol + 2 * kvw].reshape(depth, ns, ts_, N_KV_HEADS, HEAD_DIM),
            proj_s[:, ikcol:ikcol + IDX_DIM].reshape(depth, ns, ts_, IDX_DIM),
            cstate_s.reshape(depth, ns, k1, cw))
```

```python
import functools

import jax
import jax.numpy as jnp
import numpy as np
from jax import lax
from jax.experimental import pallas as pl
from jax.experimental.pallas import tpu as pltpu

F32 = jnp.float32
BF16 = jnp.bfloat16
I32 = jnp.int32

HEAD_DIM = 128
N_HEADS = 16
N_KV_HEADS = 4
GROUP = N_HEADS // N_KV_HEADS
N_IDX_HEADS = 32
IDX_DIM = 128
INDEX_TOPK = 256
CONV_K = 31
EPS = 1e-6
PAGE = 128

LANES = 128
INT_MIN = -2 ** 31
NEG_BIG = -1e30
VMEM_LIMIT = 56 * 1024 * 1024

SLOPES = tuple(2.0 ** (-8.0 * (h + 1) / N_HEADS) for h in range(N_HEADS))
IDX_SCALE = (N_IDX_HEADS ** -0.5) * (IDX_DIM ** -0.5)
ATT_SCALE = HEAD_DIM ** -0.5


def _cparams(sem):
    return pltpu.CompilerParams(dimension_semantics=sem, vmem_limit_bytes=VMEM_LIMIT)


def _ada_kernel(c_ref, w_ref, b_ref, o_ref):
    c = c_ref[...]
    a = (c * jax.nn.sigmoid(c)).astype(BF16)
    o_ref[...] = jnp.dot(a, w_ref[...].astype(BF16), preferred_element_type=F32) + b_ref[...]


def ada_matmul(c, w, b, tn=512):
    m, d = c.shape
    n = w.shape[1]
    return pl.pallas_call(
        _ada_kernel,
        grid=(n // tn,),
        in_specs=[pl.BlockSpec((m, d), lambda j: (0, 0)),
                  pl.BlockSpec((d, tn), lambda j: (0, j)),
                  pl.BlockSpec((1, tn), lambda j: (0, j))],
        out_specs=pl.BlockSpec((m, tn), lambda j: (0, j)),
        out_shape=jax.ShapeDtypeStruct((m, n), F32),
        compiler_params=_cparams(("arbitrary",)),
        name="ada_matmul",
    )(c, w, b.reshape(1, n))


def _mod_spec(mod, tm, d):
    if mod.shape[0] == 1:
        return pl.BlockSpec((1, d), lambda i: (0, 0))
    return pl.BlockSpec((tm, d), lambda i: (i, 0))


def _norm_mod_kernel(x_ref, g_ref, sh_ref, sc_ref, o_ref):
    x = x_ref[...]
    y = x * lax.rsqrt(jnp.mean(x * x, axis=-1, keepdims=True) + EPS) * g_ref[...]
    o_ref[...] = (y * (1.0 + sc_ref[...]) + sh_ref[...]).astype(o_ref.dtype)


def norm_modulate(x, g, shift, scale, out_dtype, tm=256):
    m, d = x.shape
    tm = min(tm, m)
    return pl.pallas_call(
        _norm_mod_kernel,
        grid=(m // tm,),
        in_specs=[pl.BlockSpec((tm, d), lambda i: (i, 0)),
                  pl.BlockSpec((1, d), lambda i: (0, 0)),
                  _mod_spec(shift, tm, d), _mod_spec(scale, tm, d)],
        out_specs=pl.BlockSpec((tm, d), lambda i: (i, 0)),
        out_shape=jax.ShapeDtypeStruct((m, d), out_dtype),
        compiler_params=_cparams(("parallel",)),
        name="norm_modulate",
    )(x, g.reshape(1, d), shift, scale)


def _mm_kernel(a_ref, b_ref, o_ref):
    o_ref[...] = jnp.dot(a_ref[...], b_ref[...], preferred_element_type=F32).astype(o_ref.dtype)


def matmul(a, b, tm=1024, tn=768):
    m, k = a.shape
    n = b.shape[1]
    tm = min(tm, m)
    return pl.pallas_call(
        _mm_kernel,
        grid=(m // tm, n // tn),
        in_specs=[pl.BlockSpec((tm, k), lambda i, j: (i, 0)),
                  pl.BlockSpec((k, tn), lambda i, j: (0, j))],
        out_specs=pl.BlockSpec((tm, tn), lambda i, j: (i, j)),
        out_shape=jax.ShapeDtypeStruct((m, n), F32),
        compiler_params=_cparams(("parallel", "arbitrary")),
        name="in_proj",
    )(a, b)


def _wo_kernel(a1_ref, a2_ref, w1_ref, w2_ref, x_ref, gt_ref, o_ref):
    acc = jnp.dot(a1_ref[...].astype(BF16), w1_ref[...], preferred_element_type=F32)
    acc += jnp.dot(a2_ref[...].astype(BF16), w2_ref[...], preferred_element_type=F32)
    o_ref[...] = x_ref[...] + gt_ref[...] * acc


def out_proj_residual(attn, conv, w_o, x, gate, tm=1024, tn=512):
    m, ka = attn.shape
    kc = conv.shape[1]
    assert ka == kc
    n = w_o.shape[1]
    tm = min(tm, m)
    if gate.shape[0] == 1:
        gspec = pl.BlockSpec((1, tn), lambda i, j: (0, j))
    else:
        gspec = pl.BlockSpec((tm, tn), lambda i, j: (i, j))
    return pl.pallas_call(
        _wo_kernel,
        grid=(m // tm, n // tn),
        in_specs=[pl.BlockSpec((tm, ka), lambda i, j: (i, 0)),
                  pl.BlockSpec((tm, kc), lambda i, j: (i, 0)),
                  pl.BlockSpec((ka, tn), lambda i, j: (0, j)),
                  pl.BlockSpec((kc, tn), lambda i, j: (1, j)),
                  pl.BlockSpec((tm, tn), lambda i, j: (i, j)),
                  gspec],
        out_specs=pl.BlockSpec((tm, tn), lambda i, j: (i, j)),
        out_shape=jax.ShapeDtypeStruct((m, n), F32),
        compiler_params=_cparams(("parallel", "arbitrary")),
        name="out_proj",
    )(attn, conv, w_o, w_o, x, gate)


def _mlp_kernel(h_ref, wu_ref, wd_ref, o_ref):
    @pl.when(pl.program_id(1) == 0)
    def _():
        o_ref[...] = jnp.zeros(o_ref.shape, F32)

    up = jnp.dot(h_ref[...], wu_ref[...], preferred_element_type=F32)
    act = jnp.square(jnp.maximum(up, 0.0)).astype(BF16)
    o_ref[...] += jnp.dot(act, wd_ref[...], preferred_element_type=F32)


def mlp(h, w_up, w_down, tm=512, tf=512):
    m, d = h.shape
    ff = w_up.shape[1]
    tm = min(tm, m)
    return pl.pallas_call(
        _mlp_kernel,
        grid=(m // tm, ff // tf),
        in_specs=[pl.BlockSpec((tm, d), lambda i, f: (i, 0)),
                  pl.BlockSpec((d, tf), lambda i, f: (0, f)),
                  pl.BlockSpec((tf, d), lambda i, f: (f, 0))],
        out_specs=pl.BlockSpec((tm, d), lambda i, f: (i, 0)),
        out_shape=jax.ShapeDtypeStruct((m, d), F32),
        compiler_params=_cparams(("parallel", "arbitrary")),
        name="mlp",
    )(h, w_up, w_down)


def _final_kernel(x_ref, m_ref, gt_ref, g_ref, sh_ref, sc_ref, o_ref):
    x = x_ref[...] + gt_ref[...] * m_ref[...]
    y = x * lax.rsqrt(jnp.mean(x * x, axis=-1, keepdims=True) + EPS) * g_ref[...]
    o_ref[...] = y * (1.0 + sc_ref[...]) + sh_ref[...]


def residual_final_norm(x, mlp_out, gate, g, shift, scale, tm=256):
    m, d = x.shape
    tm = min(tm, m)
    return pl.pallas_call(
        _final_kernel,
        grid=(m // tm,),
        in_specs=[pl.BlockSpec((tm, d), lambda i: (i, 0)),
                  pl.BlockSpec((tm, d), lambda i: (i, 0)),
                  _mod_spec(gate, tm, d),
                  pl.BlockSpec((1, d), lambda i: (0, 0)),
                  _mod_spec(shift, tm, d), _mod_spec(scale, tm, d)],
        out_specs=pl.BlockSpec((tm, d), lambda i: (i, 0)),
        out_shape=jax.ShapeDtypeStruct((m, d), F32),
        compiler_params=_cparams(("parallel",)),
        name="final_norm",
    )(x, mlp_out, gate, g.reshape(1, d), shift, scale)


def _ln_silu(y, lg, lb):
    mu = jnp.mean(y, axis=-1, keepdims=True)
    yc = y - mu
    var = jnp.mean(yc * yc, axis=-1, keepdims=True)
    yn = yc * lax.rsqrt(var + EPS) * lg + lb
    return yn * jax.nn.sigmoid(yn)


CONV_HALO = 32
CONV_RT = 64
CONV_CT = 512


def _conv_prompt_kernel(u_ref, w_ref, b_ref, lg_ref, lb_ref, o_ref, st_ref, zbuf, ybuf):
    tt = u_ref.shape[0]
    c = w_ref.shape[1]
    i = pl.program_id(0)

    @pl.when(i == 0)
    def _():
        zbuf[0:CONV_HALO, :] = jnp.zeros((CONV_HALO, c), F32)

    zbuf[CONV_HALO:CONV_HALO + tt, :] = u_ref[:, 0:c] * jax.nn.sigmoid(u_ref[:, c:2 * c])
    base = CONV_HALO - (CONV_K - 1)
    for rt in range(tt // CONV_RT):
        for ct in range(c // CONV_CT):
            cs = slice(ct * CONV_CT, (ct + 1) * CONV_CT)
            acc = jnp.zeros((CONV_RT, CONV_CT), F32)
            for j in range(CONV_K):
                r0 = base + j + rt * CONV_RT
                acc = acc + w_ref[j:j + 1, cs] * zbuf[r0:r0 + CONV_RT, cs]
            ybuf[rt * CONV_RT:(rt + 1) * CONV_RT, cs] = acc + b_ref[:, cs]
    o_ref[...] = _ln_silu(ybuf[...], lg_ref[...], lb_ref[...]).astype(o_ref.dtype)
    st_ref[...] = zbuf[CONV_HALO + tt - (CONV_K - 1):CONV_HALO + tt, :]
    zbuf[0:CONV_HALO, :] = zbuf[tt:tt + CONV_HALO, :]


def conv_prompt(proj, conv_w, conv_b, ln_g, ln_b, tt=256):
    t = proj.shape[0]
    c = conv_w.shape[1]
    return pl.pallas_call(
        _conv_prompt_kernel,
        grid=(t // tt,),
        in_specs=[pl.BlockSpec((tt, 2 * c), lambda i: (i, 0)),
                  pl.BlockSpec((CONV_K, c), lambda i: (0, 0)),
                  pl.BlockSpec((1, c), lambda i: (0, 0)),
                  pl.BlockSpec((1, c), lambda i: (0, 0)),
                  pl.BlockSpec((1, c), lambda i: (0, 0))],
        out_specs=[pl.BlockSpec((tt, c), lambda i: (i, 0)),
                   pl.BlockSpec((CONV_K - 1, c), lambda i: (0, 0))],
        out_shape=[jax.ShapeDtypeStruct((t, c), BF16),
                   jax.ShapeDtypeStruct((CONV_K - 1, c), F32)],
        scratch_shapes=[pltpu.VMEM((CONV_HALO + tt, c), F32), pltpu.VMEM((tt, c), F32)],
        compiler_params=_cparams(("arbitrary",)),
        name="conv_prompt",
    )(proj, conv_w, conv_b.reshape(1, c), ln_g.reshape(1, c), ln_b.reshape(1, c))


def _conv_sample_kernel(u_ref, s_ref, w_ref, b_ref, lg_ref, lb_ref, o_ref, ns_ref, zp):
    bb, t, _ = u_ref.shape
    c = w_ref.shape[1]
    k1 = CONV_K - 1
    for b in range(bb):
        zp[0:k1, :] = s_ref[b]
        zp[k1:k1 + t, :] = u_ref[b, :, 0:c] * jax.nn.sigmoid(u_ref[b, :, c:2 * c])
        acc = jnp.zeros((t, c), F32)
        for j in range(CONV_K):
            acc = acc + w_ref[j:j + 1, :] * zp[j:j + t, :]
        o_ref[b] = _ln_silu(acc + b_ref[...], lg_ref[...], lb_ref[...])
        ns_ref[b] = zp[t:t + k1, :]


def conv_sample(proj3, state, conv_w, conv_b, ln_g, ln_b, bb=8):
    n, t, _ = proj3.shape
    c = conv_w.shape[1]
    k1 = CONV_K - 1
    return pl.pallas_call(
        _conv_sample_kernel,
        grid=(n // bb,),
        in_specs=[pl.BlockSpec((bb, t, 2 * c), lambda i: (i, 0, 0)),
                  pl.BlockSpec((bb, k1, c), lambda i: (i, 0, 0)),
                  pl.BlockSpec((CONV_K, c), lambda i: (0, 0)),
                  pl.BlockSpec((1, c), lambda i: (0, 0)),
                  pl.BlockSpec((1, c), lambda i: (0, 0)),
                  pl.BlockSpec((1, c), lambda i: (0, 0))],
        out_specs=[pl.BlockSpec((bb, t, c), lambda i: (i, 0, 0)),
                   pl.BlockSpec((bb, k1, c), lambda i: (i, 0, 0))],
        out_shape=[jax.ShapeDtypeStruct((n, t, c), F32),
                   jax.ShapeDtypeStruct((n, k1, c), F32)],
        scratch_shapes=[pltpu.VMEM((k1 + t + 2, c), F32)],
        compiler_params=_cparams(("parallel",)),
        name="conv_sample",
    )(proj3, state, conv_w, conv_b.reshape(1, c), ln_g.reshape(1, c), ln_b.reshape(1, c))


def _score_key(score, valid):
    bits = pltpu.bitcast(score, I32)
    key = bits ^ ((bits >> 31) & 0x7FFFFFFF)
    key = jnp.maximum(key, INT_MIN + 1)
    return jnp.where(valid, key, INT_MIN)


def _digit_search(ok, start, nbits, bits_per_step):
    assert nbits % bits_per_step == 0

    def step(i, cur):
        unit = jnp.left_shift(jnp.int32(1), nbits - bits_per_step * (i + 1))
        digit = jnp.zeros(start.shape, I32)
        for c in range(1, 2 ** bits_per_step):
            digit = digit + jnp.where(ok(cur + c * unit), 1, 0)
        return cur + digit * unit

    return lax.fori_loop(0, nbits // bits_per_step, step, start)


def _kth_largest_key(count_ge, shape, bits_per_step=1):
    thr = _digit_search(lambda cand: count_ge(cand) >= INDEX_TOPK, jnp.full(shape, INT_MIN, I32), 32, bits_per_step)
    return jnp.maximum(thr, INT_MIN + 1)


def _tie_cut(count_ties_before, need, shape, n_pos, bits_per_step=1):
    nbits = -(-n_pos.bit_length() // bits_per_step) * bits_per_step
    return _digit_search(lambda cand: count_ties_before(cand) < need, jnp.zeros(shape, I32), nbits, bits_per_step)


IDX_TR = 32
IDX_TN = 256
SEL_RB = 64
LOG2E = 1.4426950408889634
PV_HEADS = 1


def _attn_prompt_kernel(iq_ref, q_ref, iw_ref, ikt_ref, kt_ref, v_ref, o_ref,
                        iq_r, wb, keys, keys_t, q_r, acc, m_s, maskb, thr_s, cut_s, tie_flag):
    tq = q_ref.shape[0]
    assert tq == LANES
    ts = ikt_ref.shape[2]
    qb = pl.program_id(0)
    q0 = qb * tq
    nj = (q0 + tq - 1) // ts + 1
    n_sub = tq // IDX_TR

    for s in range(n_sub):
        rs = slice(s * IDX_TR, (s + 1) * IDX_TR)
        for h in range(N_IDX_HEADS):
            hs = slice(h * IDX_TR, (h + 1) * IDX_TR)
            iq_r[s, hs, :] = iq_ref[rs, h * IDX_DIM:(h + 1) * IDX_DIM].astype(BF16)
            wb[s, hs, :] = jnp.broadcast_to(iw_ref[rs, h:h + 1] * IDX_SCALE, (IDX_TR, LANES))

    def score_chunk(j, carry):
        for s in range(n_sub):
            t_pos = q0 + s * IDX_TR + lax.broadcasted_iota(I32, (IDX_TR, LANES), 0)
            for cn in range(ts // IDX_TN):
                d = jnp.dot(iq_r[s], ikt_ref[j, :, cn * IDX_TN:(cn + 1) * IDX_TN],
                            preferred_element_type=F32)
                for half in range(IDX_TN // LANES):
                    ls = slice(half * LANES, (half + 1) * LANES)
                    sc = jnp.zeros((IDX_TR, LANES), F32)
                    for h in range(N_IDX_HEADS):
                        hs = slice(h * IDX_TR, (h + 1) * IDX_TR)
                        sc = sc + wb[s, hs, :] * jnp.maximum(d[hs, ls], 0.0)
                    c0 = cn * IDX_TN + half * LANES
                    s_pos = j * ts + c0 + lax.broadcasted_iota(I32, (IDX_TR, LANES), 1)
                    keys[j, s * IDX_TR:(s + 1) * IDX_TR, c0:c0 + LANES] = _score_key(sc, s_pos <= t_pos)
        for cn in range(ts // LANES):
            ls = slice(cn * LANES, (cn + 1) * LANES)
            keys_t[j, ls, :] = keys[j, :, ls].T
        return carry

    lax.fori_loop(0, nj, score_chunk, 0)

    def count_keys(pred):
        def body(j, cnt):
            for r in range(ts // SEL_RB):
                k = keys_t[j, r * SEL_RB:(r + 1) * SEL_RB, :]
                cnt = cnt + jnp.where(pred(k, j * ts + r * SEL_RB), 1, 0)
            return cnt
        cnt = lax.fori_loop(0, nj, body, jnp.zeros((SEL_RB, LANES), I32))
        return jnp.sum(cnt, axis=0, keepdims=True)

    thr = _kth_largest_key(lambda cand: count_keys(lambda k, s0: k >= cand), (1, LANES))
    n_ge = count_keys(lambda k, s0: k >= thr)
    has_tie = jnp.max(n_ge) > INDEX_TOPK
    tie_flag[0] = has_tie.astype(I32)

    def to_rows(x):
        return jnp.broadcast_to(x, (LANES, LANES)).T

    thr_s[...] = to_rows(thr)

    @pl.when(has_tie)
    def _():
        row = lax.broadcasted_iota(I32, (SEL_RB, LANES), 0)
        need = INDEX_TOPK - count_keys(lambda k, s0: k > thr)

        def ties_before(cand):
            return count_keys(lambda k, s0: jnp.where(k == thr, s0 + row, cand) < cand)

        cut_s[...] = to_rows(_tie_cut(ties_before, need, (1, LANES), keys.shape[0] * ts))

    for g in range(N_KV_HEADS):
        for hh in range(GROUP):
            h = g * GROUP + hh
            q_r[g, hh * tq:(hh + 1) * tq, :] = (q_ref[:, h * HEAD_DIM:(h + 1) * HEAD_DIM]
                                                * (ATT_SCALE * LOG2E)).astype(BF16)
    acc[...] = jnp.zeros(acc.shape, F32)
    m_s[...] = jnp.full(m_s.shape, -jnp.inf, F32)
    ones = jnp.ones((ts, HEAD_DIM), BF16)
    n_lt = ts // LANES

    def attend_chunk(j, carry):
        @pl.when(tie_flag[0] == 0)
        def _():
            for cn in range(n_lt):
                ls = slice(cn * LANES, (cn + 1) * LANES)
                maskb[:, ls] = jnp.where(keys[j, :, ls] >= thr_s[...], 0.0, NEG_BIG)

        @pl.when(tie_flag[0] != 0)
        def _():
            for cn in range(n_lt):
                ls = slice(cn * LANES, (cn + 1) * LANES)
                k = keys[j, :, ls]
                pos = j * ts + cn * LANES + lax.broadcasted_iota(I32, (tq, LANES), 1)
                tie_bias = jnp.where(pos <= cut_s[...], 0.0, NEG_BIG)
                maskb[:, ls] = jnp.where(k > thr_s[...], 0.0, jnp.where(k == thr_s[...], tie_bias, NEG_BIG))

        rel = (j * ts - q0 + lax.broadcasted_iota(I32, (1, ts), 1)).astype(F32)
        for g in range(N_KV_HEADS):
            s_all = jnp.dot(q_r[g], kt_ref[j, g * HEAD_DIM:(g + 1) * HEAD_DIM, :],
                            preferred_element_type=F32)
            v_aug = jnp.concatenate([v_ref[j, :, g * HEAD_DIM:(g + 1) * HEAD_DIM], ones], axis=1)
            ps, alphas = [], []
            for hh in range(GROUP):
                rows = slice(hh * tq, (hh + 1) * tq)
                lg = s_all[rows] + ((SLOPES[g * GROUP + hh] * LOG2E) * rel) + maskb[...]
                m_old = m_s[g, rows, :]
                m_new = jnp.maximum(m_old, jnp.max(lg, axis=1, keepdims=True))
                m_s[g, rows, :] = m_new
                alphas.append(jnp.exp2(m_old - m_new))
                ps.append(jnp.exp2(lg - jnp.concatenate([m_new] * n_lt, axis=1)).astype(BF16))
            for i in range(0, GROUP, PV_HEADS):
                rows = slice(i * tq, (i + PV_HEADS) * tq)
                alpha = jnp.concatenate(alphas[i:i + PV_HEADS], axis=0)
                pv = jnp.dot(jnp.concatenate(ps[i:i + PV_HEADS], axis=0), v_aug, preferred_element_type=F32)
                acc[g, rows, :] = acc[g, rows, :] * jnp.concatenate([alpha, alpha], axis=1) + pv
        return carry

    lax.fori_loop(0, nj, attend_chunk, 0)

    for h in range(N_HEADS):
        a = acc[h // GROUP, (h % GROUP) * tq:(h % GROUP + 1) * tq, :]
        o_ref[:, h * HEAD_DIM:(h + 1) * HEAD_DIM] = (a[:, :HEAD_DIM] / a[:, HEAD_DIM:]).astype(o_ref.dtype)


def attn_prompt(proj, col_iq, col_q, col_iw, ikt3, kt3, v3, tq=128):
    t = proj.shape[0]
    nc, _, ts = ikt3.shape
    kvw = kt3.shape[1]
    aw = N_HEADS * HEAD_DIM
    iqw = N_IDX_HEADS * IDX_DIM
    resident = dict(pipeline_mode=pl.Buffered(1))
    return pl.pallas_call(
        _attn_prompt_kernel,
        grid=(t // tq,),
        in_specs=[pl.BlockSpec((tq, iqw), lambda i: (i, col_iq)),
                  pl.BlockSpec((tq, aw), lambda i: (i, col_q)),
                  pl.BlockSpec((tq, LANES), lambda i: (i, col_iw)),
                  pl.BlockSpec((nc, IDX_DIM, ts), lambda i: (0, 0, 0), **resident),
                  pl.BlockSpec((nc, kvw, ts), lambda i: (0, 0, 0), **resident),
                  pl.BlockSpec((nc, ts, kvw), lambda i: (0, 0, 0), **resident)],
        out_specs=pl.BlockSpec((tq, aw), lambda i: (i, 0)),
        out_shape=jax.ShapeDtypeStruct((t, aw), BF16),
        scratch_shapes=[pltpu.VMEM((tq // IDX_TR, N_IDX_HEADS * IDX_TR, IDX_DIM), BF16),
                        pltpu.VMEM((tq // IDX_TR, N_IDX_HEADS * IDX_TR, LANES), F32),
                        pltpu.VMEM((nc, tq, ts), I32),
                        pltpu.VMEM((nc, ts, tq), I32),
                        pltpu.VMEM((N_KV_HEADS, GROUP * tq, HEAD_DIM), BF16),
                        pltpu.VMEM((N_KV_HEADS, GROUP * tq, 2 * HEAD_DIM), F32),
                        pltpu.VMEM((N_KV_HEADS, GROUP * tq, LANES), F32),
                        pltpu.VMEM((tq, ts), F32),
                        pltpu.VMEM((tq, LANES), I32),
                        pltpu.VMEM((tq, LANES), I32),
                        pltpu.SMEM((1,), I32)],
        compiler_params=_cparams(("parallel",)),
        name="attn_prompt",
    )(proj, proj, proj, ikt3, kt3, v3)


NT_DIMS = (((1,), (1,)), ((), ()))
SAMPLE_SEARCH_BITS = 4


def _attn_sample_kernel(pt_ref, iq_ref, q_ref, iw_ref, kn_ref, vn_ref, ikn_ref, slope_ref, *refs):
    n_pages = (len(refs) - 2) // 3
    ik_refs, k_refs, v_refs = refs[:n_pages], refs[n_pages:2 * n_pages], refs[2 * n_pages:3 * n_pages]
    o_ref, lg_s = refs[3 * n_pages], refs[3 * n_pages + 1]
    t = q_ref.shape[1]
    past = n_pages * PAGE
    vk = PAGE * N_KV_HEADS
    rows = N_HEADS * t

    iq_r = jnp.concatenate([iq_ref[0, :, h * IDX_DIM:(h + 1) * IDX_DIM] for h in range(N_IDX_HEADS)],
                           axis=0).astype(BF16)
    wb = jnp.concatenate([jnp.broadcast_to(iw_ref[0, :, h:h + 1] * IDX_SCALE, (t, LANES))
                          for h in range(N_IDX_HEADS)], axis=0)

    def page_scores(ik_page):
        d = lax.dot_general(iq_r, ik_page, NT_DIMS, preferred_element_type=F32)
        sc = jnp.zeros((t, PAGE), F32)
        for h in range(N_IDX_HEADS):
            sc = sc + wb[h * t:(h + 1) * t, :] * jnp.maximum(d[h * t:(h + 1) * t, :], 0.0)
        return sc

    lane_t = lax.broadcasted_iota(I32, (t, PAGE), 1)
    row_t = lax.broadcasted_iota(I32, (t, PAGE), 0)
    keys = [_score_key(page_scores(ik_refs[p][0].astype(BF16)), lane_t >= 0) for p in range(n_pages)]
    ik_new = jnp.concatenate([ikn_ref[0], jnp.zeros((PAGE - t, IDX_DIM), F32)], axis=0)
    keys.append(_score_key(page_scores(ik_new.astype(BF16)), lane_t <= row_t))

    def count(pred):
        cnt = jnp.zeros((t, LANES), I32)
        for p, k in enumerate(keys):
            cnt = cnt + jnp.where(pred(k, p * PAGE + lane_t), 1, 0)
        return jnp.sum(cnt, axis=1, keepdims=True)

    thr = _kth_largest_key(lambda cand: count(lambda k, pos: k >= cand), (t, LANES), SAMPLE_SEARCH_BITS)
    need = INDEX_TOPK - count(lambda k, pos: k > thr)
    cut = _tie_cut(lambda cand: count(lambda k, pos: jnp.where(k == thr, pos, cand) < cand),
                   need, (t, LANES), past + PAGE, SAMPLE_SEARCH_BITS)

    def select_bias(k, pos):
        tie = jnp.where(pos <= cut, 0.0, NEG_BIG)
        return jnp.where(k > thr, 0.0, jnp.where(k == thr, tie, NEG_BIG)).astype(BF16)

    q_all = jnp.concatenate([q_ref[0, :, h * HEAD_DIM:(h + 1) * HEAD_DIM] for h in range(N_HEADS)],
                            axis=0).astype(BF16)
    slope = slope_ref[...]

    def tiled(x, n):
        return jnp.concatenate([x] * n, axis=1)

    r_i = lax.broadcasted_iota(I32, (rows, vk), 0)
    c_i = lax.broadcasted_iota(I32, (rows, vk), 1)
    head_ok = (r_i // (t * GROUP)) == (c_i % N_KV_HEADS)
    tok = lax.broadcasted_iota(I32, (PAGE, vk), 0)
    expand = jnp.where(tok == lax.broadcasted_iota(I32, (PAGE, vk), 1) // N_KV_HEADS, 1.0, 0.0).astype(BF16)
    rel0 = (lax.broadcasted_iota(I32, (1, vk), 1) // N_KV_HEADS - past).astype(F32)
    base = tiled(slope, vk // LANES) * rel0 + jnp.where(head_ok, 0.0, NEG_BIG)

    mx = jnp.full((rows, vk), -jnp.inf, F32)
    for p in range(n_pages):
        s = lax.dot_general(q_all, k_refs[p][0].astype(BF16), NT_DIMS, preferred_element_type=F32)
        sel = jnp.dot(select_bias(keys[p], p * PAGE + lane_t), expand, preferred_element_type=F32)
        lg = s * ATT_SCALE + base + tiled(slope * float(p * PAGE), vk // LANES) \
            + jnp.concatenate([sel] * N_HEADS, axis=0)
        lg_s[p] = lg
        mx = jnp.maximum(mx, lg)

    def nk(ref):
        return jnp.concatenate([ref[0, :, g * HEAD_DIM:(g + 1) * HEAD_DIM] for g in range(N_KV_HEADS)]
                               + [jnp.zeros((PAGE - N_KV_HEADS * t, HEAD_DIM), F32)], axis=0).astype(BF16)

    rn =lax.broadcasted_iota(I32, (rows, PAGE), 0)
    cn = lax.broadcasted_iota(I32, (rows, PAGE), 1)
    head_ok_n = ((rn // (t * GROUP)) == (cn // t)) & (cn < N_KV_HEADS * t)
    tok_n = lax.broadcasted_iota(I32, (PAGE, PAGE), 0)
    col_n = lax.broadcasted_iota(I32, (PAGE, PAGE), 1)
    expand_n = jnp.where((tok_n == col_n % t) & (col_n < N_KV_HEADS * t), 1.0, 0.0).astype(BF16)
    rel_n = (lax.broadcasted_iota(I32, (1, PAGE), 1) % t).astype(F32)
    s_n = lax.dot_general(q_all, nk(kn_ref), NT_DIMS, preferred_element_type=F32)
    sel_n = jnp.dot(select_bias(keys[n_pages], past + lane_t), expand_n, preferred_element_type=F32)
    lg_n = s_n * ATT_SCALE + slope * rel_n + jnp.where(head_ok_n, 0.0, NEG_BIG) \
        + jnp.concatenate([sel_n] * N_HEADS, axis=0)

    m_row = jnp.maximum(jnp.max(mx, axis=1, keepdims=True), jnp.max(lg_n, axis=1, keepdims=True))
    pr_n = jnp.exp(lg_n - m_row)
    num = jnp.dot(pr_n.astype(BF16), nk(vn_ref), preferred_element_type=F32)
    den_n = jnp.sum(pr_n, axis=1, keepdims=True)
    den = jnp.zeros((rows, vk), F32)
    for p in range(n_pages):
        pr = jnp.exp(lg_s[p] - m_row)
        den = den + pr
        num = num + jnp.dot(pr.astype(BF16), v_refs[p][0].astype(BF16), preferred_element_type=F32)
    out = num / (jnp.sum(den, axis=1, keepdims=True) + den_n)
    for h in range(N_HEADS):
        o_ref[0, :, h * HEAD_DIM:(h + 1) * HEAD_DIM] = out[h * t:(h + 1) * t, :]


def attn_sample(page_table, proj3, cols, cache_k, cache_v, cache_ik):
    n, t, _ = proj3.shape
    n_pages = page_table.shape[1]
    kvw = N_KV_HEADS * HEAD_DIM
    aw = N_HEADS * HEAD_DIM
    iqw = N_IDX_HEADS * IDX_DIM
    vk = PAGE * N_KV_HEADS
    slope_rows = jnp.asarray(np.repeat(np.asarray(SLOPES, np.float32), t)[:, None]
                             * np.ones((1, LANES), np.float32))

    def pspec(width, col):
        return pl.BlockSpec((1, t, width), lambda b, pt: (b, 0, col))

    def cspec(rows, width, p):
        return pl.BlockSpec((1, rows, width), lambda b, pt: (pt[b, p], 0, 0))

    grid_spec = pltpu.PrefetchScalarGridSpec(
        num_scalar_prefetch=1,
        grid=(n,),
        in_specs=[pspec(iqw, cols["iq"]), pspec(aw, cols["q"]), pspec(LANES, cols["iw"]),
                  pspec(kvw, cols["k"]), pspec(kvw, cols["v"]), pspec(IDX_DIM, cols["ik"]),
                  pl.BlockSpec((N_HEADS * t, LANES), lambda b, pt: (0, 0))]
        + [cspec(PAGE, IDX_DIM, p) for p in range(n_pages)]
        + [cspec(vk, HEAD_DIM, p) for p in range(n_pages)]
        + [cspec(vk, HEAD_DIM, p) for p in range(n_pages)],
        out_specs=pl.BlockSpec((1, t, aw), lambda b, pt: (b, 0, 0)),
        scratch_shapes=[pltpu.VMEM((n_pages, N_HEADS * t, vk), F32)],
    )
    return pl.pallas_call(
        _attn_sample_kernel,
        grid_spec=grid_spec,
        out_shape=jax.ShapeDtypeStruct((n, t, aw), F32),
        compiler_params=_cparams(("parallel",)),
        name="attn_sample",
    )(page_table, proj3, proj3, proj3, proj3, proj3, proj3, slope_rows,
      *([cache_ik] * n_pages), *([cache_k] * n_pages), *([cache_v] * n_pages))


ATT_TS = 512


def _layer_weights(w_in, w_o, w_up, w_down, d_model):
    aw = N_HEADS * HEAD_DIM
    kvw = N_KV_HEADS * HEAD_DIM
    iqw = N_IDX_HEADS * IDX_DIM
    sizes = (aw, kvw, kvw, iqw, IDX_DIM, N_IDX_HEADS, w_in.shape[1] - (aw + 2 * kvw + iqw + IDX_DIM + N_IDX_HEADS))
    offs = [0]
    for s in sizes:
        offs.append(offs[-1] + s)
    wq, wk, wv, wiq, wik, wiw, wu = (w_in[:, offs[i]:offs[i + 1]] for i in range(7))
    wiw = jnp.pad(wiw, ((0, 0), (0, LANES - N_IDX_HEADS)))
    w_proj = jnp.concatenate([wu, wiq, wq, wk, wv, wik, wiw], axis=1).astype(BF16)
    cu = sizes[6]
    assert cu == iqw and cu % aw == 0 and (cu + iqw + aw) % kvw == 0
    cols = {"u": 0, "iq": cu // iqw, "q": (cu + iqw) // aw, "k": (cu + iqw + aw) // kvw,
            "v": (cu + iqw + aw) // kvw + 1, "ik": (cu + iqw + aw + 2 * kvw) // IDX_DIM,
            "iw": (cu + iqw + aw + 2 * kvw) // IDX_DIM + 1}
    return w_proj, cols, w_o.astype(BF16), w_up.astype(BF16), w_down.astype(BF16)


def kernel(x_prompt, x_sample, cache_k, cache_v, cache_idx_k, state_conv, page_table, c_prompt, c_sample, w_ada, b_ada, g_mix, w_in, w_o, conv_w, conv_b, conv_ln_g, conv_ln_b, g_mlp, w_up, w_down, w_ada_final, b_ada_final, g_final):
    nb, seq, d = x_prompt.shape
    ns, ts_, _ = x_sample.shape
    depth = w_ada.shape[0]
    assert nb == 1 and depth == 1
    kvw = N_KV_HEADS * HEAD_DIM
    cw = conv_w.shape[2]
    k1 = CONV_K - 1

    n_c = ns + nb
    pad = (-n_c) % 16
    c_all = jnp.concatenate([c_sample, c_prompt, jnp.zeros((pad, d), F32)], axis=0)
    mods = ada_matmul(c_all, w_ada[0], b_ada[0])
    mods_f = ada_matmul(c_all, w_ada_final, b_ada_final)

    def mod_p(arr, idx):
        return arr[ns:ns + 1, idx * d:(idx + 1) * d]

    def mod_s(arr, idx):
        return jnp.repeat(arr[:ns, idx * d:(idx + 1) * d], ts_, axis=0)

    w_proj, cols, w_o_b, w_up_b, w_down_b = _layer_weights(w_in[0], w_o[0], w_up[0], w_down[0], d)
    kcol = cols["k"] * kvw
    ikcol = cols["ik"] * IDX_DIM

    def trunk(x2, mod, attn, conv):
        x1 = out_proj_residual(attn, conv, w_o_b, x2, mod(mods, 2))
        h2 = norm_modulate(x1, g_mlp[0], mod(mods, 3), mod(mods, 4), BF16)
        m = mlp(h2, w_up_b, w_down_b)
        return residual_final_norm(x1, m, mod(mods, 5), g_final, mod(mods_f, 0), mod(mods_f, 1))

    xp = x_prompt.reshape(seq, d)
    hp = norm_modulate(xp, g_mix[0], mod_p(mods, 0), mod_p(mods, 1), BF16)
    proj_p = matmul(hp, w_proj)
    k_p = proj_p[:, kcol:kcol + kvw]
    v_p = proj_p[:, kcol + kvw:kcol + 2 * kvw]
    ik_p = proj_p[:, ikcol:ikcol + IDX_DIM]
    nc = seq // ATT_TS
    ikt3 = ik_p.astype(BF16).reshape(nc, ATT_TS, IDX_DIM).transpose(0, 2, 1)
    kt3 = k_p.astype(BF16).reshape(nc, ATT_TS, kvw).transpose(0, 2, 1)
    v3 = v_p.astype(BF16).reshape(nc, ATT_TS, kvw)
    attn_p = attn_prompt(proj_p, cols["iq"], cols["q"], cols["iw"], ikt3, kt3, v3)
    conv_p, cstate_p = conv_prompt(proj_p, conv_w[0], conv_b[0], conv_ln_g[0], conv_ln_b[0])
    y_p = trunk(xp, mod_p, attn_p, conv_p)

    xs = x_sample.reshape(ns * ts_, d)
    hs = norm_modulate(xs, g_mix[0], mod_s(mods, 0), mod_s(mods, 1), BF16)
    proj_s = matmul(hs, w_proj)
    proj_s3 = proj_s.reshape(ns, ts_, proj_s.shape[1])
    n_pool = cache_k.shape[1]
    attn_s = attn_sample(page_table, proj_s3, cols,
                         cache_k[0].reshape(n_pool, PAGE * N_KV_HEADS, HEAD_DIM),
                         cache_v[0].reshape(n_pool, PAGE * N_KV_HEADS, HEAD_DIM), cache_idx_k[0])
    conv_s, cstate_s = conv_sample(proj_s3, state_conv[0], conv_w[0], conv_b[0], conv_ln_g[0], conv_ln_b[0])
    y_s = trunk(xs, mod_s, attn_s.reshape(ns * ts_, -1).astype(BF16), conv_s.reshape(ns * ts_, cw).astype(BF16))

    return (y_p.reshape(nb, seq, d),
            y_s.reshape(ns, ts_, d),
            k_p.reshape(depth, nb, seq, N_KV_HEADS, HEAD_DIM),
            v_p.reshape(depth, nb, seq, N_KV_HEADS, HEAD_DIM),
            ik_p.reshape(depth, nb, seq, IDX_DIM),
            cstate_p.reshape(depth, nb, k1, cw),
            proj_s[:, kcol:kcol + kvw].reshape(depth, ns, ts_, N_KV_HEADS, HEAD_DIM),
            proj_s[:, kcol + kvw:kcol + 2 * kvw].reshape(depth, ns, ts_, N_KV_HEADS, HEAD_DIM),
            proj_s[:, ikcol:ikcol + IDX_DIM].reshape(depth, ns, ts_, IDX_DIM),
            cstate_s.reshape(depth, ns, k1, cw))
```

```python
import functools

import jax
import jax.numpy as jnp
import numpy as np
from jax import lax
from jax.experimental import pallas as pl
from jax.experimental.pallas import tpu as pltpu

F32 = jnp.float32
BF16 = jnp.bfloat16
I32 = jnp.int32

HEAD_DIM = 128
N_HEADS = 16
N_KV_HEADS = 4
GROUP = N_HEADS // N_KV_HEADS
N_IDX_HEADS = 32
IDX_DIM = 128
INDEX_TOPK = 256
CONV_K = 31
EPS = 1e-6
PAGE = 128

LANES = 128
SUBLANES = 8
INT_MIN = -2 ** 31
NEG_BIG = -1e30
VMEM_LIMIT = 56 * 1024 * 1024

SLOPES = tuple(2.0 ** (-8.0 * (h + 1) / N_HEADS) for h in range(N_HEADS))
IDX_SCALE = (N_IDX_HEADS ** -0.5) * (IDX_DIM ** -0.5)
ATT_SCALE = HEAD_DIM ** -0.5


def _cparams(sem):
    return pltpu.CompilerParams(dimension_semantics=sem, vmem_limit_bytes=VMEM_LIMIT)


def _ada_kernel(c_ref, w_ref, b_ref, o_ref):
    c = c_ref[...]
    a = (c * jax.nn.sigmoid(c)).astype(BF16)
    o_ref[...] = jnp.dot(a, w_ref[...].astype(BF16), preferred_element_type=F32) + b_ref[...]


def ada_matmul(c, w, b, tn=512):
    m, d = c.shape
    n = w.shape[1]
    return pl.pallas_call(
        _ada_kernel,
        grid=(n // tn,),
        in_specs=[pl.BlockSpec((m, d), lambda j: (0, 0)),
                  pl.BlockSpec((d, tn), lambda j: (0, j)),
                  pl.BlockSpec((1, tn), lambda j: (0, j))],
        out_specs=pl.BlockSpec((m, tn), lambda j: (0, j)),
        out_shape=jax.ShapeDtypeStruct((m, n), F32),
        compiler_params=_cparams(("arbitrary",)),
        name="ada_matmul",
    )(c, w, b.reshape(1, n))


def _mod_spec(mod, tm, d):
    if mod.shape[0] == 1:
        return pl.BlockSpec((1, d), lambda i: (0, 0))
    return pl.BlockSpec((tm, d), lambda i: (i, 0))


def _norm_mod_kernel(x_ref, g_ref, sh_ref, sc_ref, o_ref):
    x = x_ref[...]
    y = x * lax.rsqrt(jnp.mean(x * x, axis=-1, keepdims=True) + EPS) * g_ref[...]
    o_ref[...] = (y * (1.0 + sc_ref[...]) + sh_ref[...]).astype(o_ref.dtype)


def norm_modulate(x, g, shift, scale, out_dtype, tm=256):
    m, d = x.shape
    tm = min(tm, m)
    return pl.pallas_call(
        _norm_mod_kernel,
        grid=(m // tm,),
        in_specs=[pl.BlockSpec((tm, d), lambda i: (i, 0)),
                  pl.BlockSpec((1, d), lambda i: (0, 0)),
                  _mod_spec(shift, tm, d), _mod_spec(scale, tm, d)],
        out_specs=pl.BlockSpec((tm, d), lambda i: (i, 0)),
        out_shape=jax.ShapeDtypeStruct((m, d), out_dtype),
        compiler_params=_cparams(("parallel",)),
        name="norm_modulate",
    )(x, g.reshape(1, d), shift, scale)


def _in_proj_kernel(a_ref, b_ref, s_ref, lo_ref, hi_ref, *, n_lo):
    j = pl.program_id(1)
    acc = jnp.dot(a_ref[...], b_ref[...], preferred_element_type=F32)

    @pl.when(j < n_lo)
    def _():
        lo_ref[...] = (acc * s_ref[...]).astype(lo_ref.dtype)

    @pl.when(j >= n_lo)
    def _():
        hi_ref[...] = acc


def in_proj(a, b, col_scale, tm=1024, tn=768):
    m, k = a.shape
    n = b.shape[1]
    n_bf16 = col_scale.shape[1]
    tm = min(tm, m)
    n_lo = n_bf16 // tn
    assert n_lo * tn == n_bf16 and n % tn == 0
    return pl.pallas_call(
        functools.partial(_in_proj_kernel, n_lo=n_lo),
        grid=(m // tm, n // tn),
        in_specs=[pl.BlockSpec((tm, k), lambda i, j: (i, 0)),
                  pl.BlockSpec((k, tn), lambda i, j: (0, j)),
                  pl.BlockSpec((1, tn), lambda i, j: (0, jnp.minimum(j, n_lo - 1)))],
        out_specs=[pl.BlockSpec((tm, tn), lambda i, j: (i, jnp.minimum(j, n_lo - 1))),
                   pl.BlockSpec((tm, tn), lambda i, j: (i, jnp.maximum(j - n_lo, 0)))],
        out_shape=[jax.ShapeDtypeStruct((m, n_bf16), BF16),
                   jax.ShapeDtypeStruct((m, n - n_bf16), F32)],
        compiler_params=_cparams(("parallel", "arbitrary")),
        name="in_proj",
    )(a, b, col_scale)


def _wo_kernel(a1_ref, a2_ref, w1_ref, w2_ref, x_ref, gt_ref, o_ref):
    acc = jnp.dot(a1_ref[...].astype(BF16), w1_ref[...], preferred_element_type=F32)
    acc += jnp.dot(a2_ref[...].astype(BF16), w2_ref[...], preferred_element_type=F32)
    o_ref[...] = x_ref[...] + gt_ref[...] * acc


def out_proj_residual(attn, conv, w_o, x, gate, tm=1024, tn=512):
    m, ka = attn.shape
    kc = conv.shape[1]
    assert ka == kc
    n = w_o.shape[1]
    tm = min(tm, m)
    if gate.shape[0] == 1:
        gspec = pl.BlockSpec((1, tn), lambda i, j: (0, j))
    else:
        gspec = pl.BlockSpec((tm, tn), lambda i, j: (i, j))
    return pl.pallas_call(
        _wo_kernel,
        grid=(m // tm, n // tn),
        in_specs=[pl.BlockSpec((tm, ka), lambda i, j: (i, 0)),
                  pl.BlockSpec((tm, kc), lambda i, j: (i, 0)),
                  pl.BlockSpec((ka, tn), lambda i, j: (0, j)),
                  pl.BlockSpec((kc, tn), lambda i, j: (1, j)),
                  pl.BlockSpec((tm, tn), lambda i, j: (i, j)),
                  gspec],
        out_specs=pl.BlockSpec((tm, tn), lambda i, j: (i, j)),
        out_shape=jax.ShapeDtypeStruct((m, n), F32),
        compiler_params=_cparams(("parallel", "arbitrary")),
        name="out_proj",
    )(attn, conv, w_o, w_o, x, gate)


def _mlp_kernel(h_ref, wu_ref, wd_ref, o_ref):
    @pl.when(pl.program_id(1) == 0)
    def _():
        o_ref[...] = jnp.zeros(o_ref.shape, F32)

    up = jnp.dot(h_ref[...], wu_ref[...], preferred_element_type=F32)
    act = jnp.square(jnp.maximum(up, 0.0)).astype(BF16)
    o_ref[...] += jnp.dot(act, wd_ref[...], preferred_element_type=F32)


def mlp(h, w_up, w_down, tm=512, tf=512):
    m, d = h.shape
    ff = w_up.shape[1]
    tm = min(tm, m)
    return pl.pallas_call(
        _mlp_kernel,
        grid=(m // tm, ff // tf),
        in_specs=[pl.BlockSpec((tm, d), lambda i, f: (i, 0)),
                  pl.BlockSpec((d, tf), lambda i, f: (0, f)),
                  pl.BlockSpec((tf, d), lambda i, f: (f, 0))],
        out_specs=pl.BlockSpec((tm, d), lambda i, f: (i, 0)),
        out_shape=jax.ShapeDtypeStruct((m, d), F32),
        compiler_params=_cparams(("parallel", "arbitrary")),
        name="mlp",
    )(h, w_up, w_down)


def _final_kernel(x_ref, m_ref, gt_ref, g_ref, sh_ref, sc_ref, o_ref):
    x = x_ref[...] + gt_ref[...] * m_ref[...]
    y = x * lax.rsqrt(jnp.mean(x * x, axis=-1, keepdims=True) + EPS) * g_ref[...]
    o_ref[...] = y * (1.0 + sc_ref[...]) + sh_ref[...]


def residual_final_norm(x, mlp_out, gate, g, shift, scale, tm=256):
    m, d = x.shape
    tm = min(tm, m)
    return pl.pallas_call(
        _final_kernel,
        grid=(m // tm,),
        in_specs=[pl.BlockSpec((tm, d), lambda i: (i, 0)),
                  pl.BlockSpec((tm, d), lambda i: (i, 0)),
                  _mod_spec(gate, tm, d),
                  pl.BlockSpec((1, d), lambda i: (0, 0)),
                  _mod_spec(shift, tm, d), _mod_spec(scale, tm, d)],
        out_specs=pl.BlockSpec((tm, d), lambda i: (i, 0)),
        out_shape=jax.ShapeDtypeStruct((m, d), F32),
        compiler_params=_cparams(("parallel",)),
        name="final_norm",
    )(x, mlp_out, gate, g.reshape(1, d), shift, scale)


def _ln_silu(y, lg, lb):
    mu = jnp.mean(y, axis=-1, keepdims=True)
    yc = y - mu
    var = jnp.mean(yc * yc, axis=-1, keepdims=True)
    yn = yc * lax.rsqrt(var + EPS) * lg + lb
    return yn * jax.nn.sigmoid(yn)


CONV_HALO = 32
CONV_RT = 64
CONV_CT = 512


def _conv_prompt_kernel(u_ref, w_ref, b_ref, lg_ref, lb_ref, o_ref, st_ref, zbuf, ybuf, zs):
    tt = u_ref.shape[0]
    c = w_ref.shape[1]
    i = pl.program_id(0)

    @pl.when(i == 0)
    def _():
        zbuf[0:CONV_HALO, :] = jnp.zeros((CONV_HALO, c), F32)

    zbuf[CONV_HALO:CONV_HALO + tt, :] = u_ref[:, 0:c] * jax.nn.sigmoid(u_ref[:, c:2 * c])
    zs_rows = zs.shape[1]
    for r in range(1, SUBLANES):
        zs[r - 1] = zbuf[r:r + zs_rows, :]
    base = CONV_HALO - (CONV_K - 1)
    for rt in range(tt // CONV_RT):
        for ct in range(c // CONV_CT):
            cs = slice(ct * CONV_CT, (ct + 1) * CONV_CT)
            acc = jnp.zeros((CONV_RT, CONV_CT), F32)
            for j in range(CONV_K):
                a, r = divmod(base + j, SUBLANES)
                r0 = a * SUBLANES + rt * CONV_RT
                window = zbuf[r0:r0 + CONV_RT, cs] if r == 0 else zs[r - 1, r0:r0 + CONV_RT, cs]
                acc = acc + w_ref[j:j + 1, cs] * window
            ybuf[rt * CONV_RT:(rt + 1) * CONV_RT, cs] = acc + b_ref[:, cs]
    o_ref[...] = _ln_silu(ybuf[...], lg_ref[...], lb_ref[...]).astype(o_ref.dtype)
    st_ref[...] = zbuf[CONV_HALO + tt - (CONV_K - 1):CONV_HALO + tt, :]
    zbuf[0:CONV_HALO, :] = zbuf[tt:tt + CONV_HALO, :]


def conv_prompt(proj, conv_w, conv_b, ln_g, ln_b, tt=256):
    t = proj.shape[0]
    c = conv_w.shape[1]
    return pl.pallas_call(
        _conv_prompt_kernel,
        grid=(t // tt,),
        in_specs=[pl.BlockSpec((tt, 2 * c), lambda i: (i, 0)),
                  pl.BlockSpec((CONV_K, c), lambda i: (0, 0)),
                  pl.BlockSpec((1, c), lambda i: (0, 0)),
                  pl.BlockSpec((1, c), lambda i: (0, 0)),
                  pl.BlockSpec((1, c), lambda i: (0, 0))],
        out_specs=[pl.BlockSpec((tt, c), lambda i: (i, 0)),
                   pl.BlockSpec((CONV_K - 1, c), lambda i: (0, 0))],
        out_shape=[jax.ShapeDtypeStruct((t, c), BF16),
                   jax.ShapeDtypeStruct((CONV_K - 1, c), F32)],
        scratch_shapes=[pltpu.VMEM((CONV_HALO + tt, c), F32), pltpu.VMEM((tt, c), F32),
                        pltpu.VMEM((SUBLANES - 1, CONV_HALO + tt - SUBLANES, c), F32)],
        compiler_params=_cparams(("arbitrary",)),
        name="conv_prompt",
    )(proj, conv_w, conv_b.reshape(1, c), ln_g.reshape(1, c), ln_b.reshape(1, c))


def _conv_sample_kernel(u_ref, s_ref, w_ref, b_ref, lg_ref, lb_ref, o_ref, ns_ref, zp):
    bb, t, _ = u_ref.shape
    c = w_ref.shape[1]
    k1 = CONV_K - 1
    for b in range(bb):
        zp[0:k1, :] = s_ref[b]
        zp[k1:k1 + t, :] = u_ref[b, :, 0:c] * jax.nn.sigmoid(u_ref[b, :, c:2 * c])
        acc = jnp.zeros((t, c), F32)
        for j in range(CONV_K):
            acc = acc + w_ref[j:j + 1, :] * zp[j:j + t, :]
        o_ref[b] = _ln_silu(acc + b_ref[...], lg_ref[...], lb_ref[...])
        ns_ref[b] = zp[t:t + k1, :]


def conv_sample(proj3, state, conv_w, conv_b, ln_g, ln_b, bb=8):
    n, t, _ = proj3.shape
    c = conv_w.shape[1]
    k1 = CONV_K - 1
    return pl.pallas_call(
        _conv_sample_kernel,
        grid=(n // bb,),
        in_specs=[pl.BlockSpec((bb, t, 2 * c), lambda i: (i, 0, 0)),
                  pl.BlockSpec((bb, k1, c), lambda i: (i, 0, 0)),
                  pl.BlockSpec((CONV_K, c), lambda i: (0, 0)),
                  pl.BlockSpec((1, c), lambda i: (0, 0)),
                  pl.BlockSpec((1, c), lambda i: (0, 0)),
                  pl.BlockSpec((1, c), lambda i: (0, 0))],
        out_specs=[pl.BlockSpec((bb, t, c), lambda i: (i, 0, 0)),
                   pl.BlockSpec((bb, k1, c), lambda i: (i, 0, 0))],
        out_shape=[jax.ShapeDtypeStruct((n, t, c), F32),
                   jax.ShapeDtypeStruct((n, k1, c), F32)],
        scratch_shapes=[pltpu.VMEM((k1 + t + 2, c), F32)],
        compiler_params=_cparams(("parallel",)),
        name="conv_sample",
    )(proj3, state, conv_w, conv_b.reshape(1, c), ln_g.reshape(1, c), ln_b.reshape(1, c))


def _score_key(score, valid):
    bits = pltpu.bitcast(score, I32)
    key = bits ^ ((bits >> 31) & 0x7FFFFFFF)
    key = jnp.maximum(key, INT_MIN + 1)
    return jnp.where(valid, key, INT_MIN)


def _digit_search(ok, start, nbits, bits_per_step):
    assert nbits % bits_per_step == 0

    def step(i, cur):
        unit = jnp.left_shift(jnp.int32(1), nbits - bits_per_step * (i + 1))
        digit = jnp.zeros(start.shape, I32)
        for c in range(1, 2 ** bits_per_step):
            digit = digit + jnp.where(ok(cur + c * unit), 1, 0)
        return cur + digit * unit

    return lax.fori_loop(0, nbits // bits_per_step, step, start)


def _kth_largest_key(count_ge, shape, bits_per_step=1):
    thr = _digit_search(lambda cand: count_ge(cand) >= INDEX_TOPK, jnp.full(shape, INT_MIN, I32), 32, bits_per_step)
    return jnp.maximum(thr, INT_MIN + 1)


def _tie_cut(count_ties_before, need, shape, n_pos, bits_per_step=1):
    nbits = -(-n_pos.bit_length() // bits_per_step) * bits_per_step
    return _digit_search(lambda cand: count_ties_before(cand) < need, jnp.zeros(shape, I32), nbits, bits_per_step)


IDX_TR = 128
IDX_TN = 256
SEL_RB = 64
LOG2E = 1.4426950408889634
INT_MAX = 2 ** 31 - 1


def _attn_prompt_kernel(iq_ref, q_ref, iw_ref, ikt_ref, kt_ref, v_ref, o_ref,
                        iq_r, wb, ktile, keys_t, q_r, acc, m_s, maskb, cut_s):
    tq = q_ref.shape[0]
    ts = ikt_ref.shape[2]
    qb = pl.program_id(0)
    q0 = qb * tq
    nj = (q0 + tq - 1) // ts + 1
    n_sub = tq // IDX_TR
    n_qt = tq // LANES

    for s in range(n_sub):
        rs = slice(s * IDX_TR, (s + 1) * IDX_TR)
        for h in range(N_IDX_HEADS):
            hs = slice(h * IDX_TR, (h + 1) * IDX_TR)
            iq_r[s, hs, :] = iq_ref[rs, h * IDX_DIM:(h + 1) * IDX_DIM]
            wb[s, hs, :] = jnp.broadcast_to(iw_ref[rs, h:h + 1] * IDX_SCALE, (IDX_TR, LANES))

    def score_chunk(j, carry):
        for s in range(n_sub):
            t_pos = q0 + s * IDX_TR + lax.broadcasted_iota(I32, (IDX_TR, LANES), 0)
            for cn in range(ts // IDX_TN):
                d = jnp.dot(iq_r[s], ikt_ref[j, :, cn * IDX_TN:(cn + 1) * IDX_TN],
                            preferred_element_type=F32)
                for half in range(IDX_TN // LANES):
                    ls = slice(half * LANES, (half + 1) * LANES)
                    sc = jnp.zeros((IDX_TR, LANES), F32)
                    for h in range(N_IDX_HEADS):
                        hs = slice(h * IDX_TR, (h + 1) * IDX_TR)
                        sc = sc + wb[s, hs, :] * jnp.maximum(d[hs, ls], 0.0)
                    c0 = cn * IDX_TN + half * LANES
                    s_pos = j * ts + c0 + lax.broadcasted_iota(I32, (IDX_TR, LANES), 1)
                    ktile[s * IDX_TR:(s + 1) * IDX_TR, c0:c0 + LANES] = _score_key(sc, s_pos <= t_pos)
        for qt in range(n_qt):
            for cn in range(ts // LANES):
                keys_t[j, cn * LANES:(cn + 1) * LANES, qt * LANES:(qt + 1) * LANES] = \
                    ktile[qt * LANES:(qt + 1) * LANES, cn * LANES:(cn + 1) * LANES].T
        return carry

    lax.fori_loop(0, nj, score_chunk, 0)

    def count_keys(pred):
        def body(j, cnt):
            for r in range(ts // SEL_RB):
                k = keys_t[j, r * SEL_RB:(r + 1) * SEL_RB, :]
                cnt = cnt + jnp.where(pred(k, j * ts + r * SEL_RB), 1, 0)
            return cnt
        cnt = lax.fori_loop(0, nj, body, jnp.zeros((SEL_RB, tq), I32))
        return jnp.sum(cnt, axis=0, keepdims=True)

    thr = _kth_largest_key(lambda cand: count_keys(lambda k, s0: k >= cand), (1, tq))
    n_ge = count_keys(lambda k, s0: k >= thr)
    cut_s[...] = jnp.full(cut_s.shape, INT_MAX, I32)

    @pl.when(jnp.max(n_ge) > INDEX_TOPK)
    def _():
        row = lax.broadcasted_iota(I32, (SEL_RB, tq), 0)
        need = INDEX_TOPK - count_keys(lambda k, s0: k > thr)

        def ties_before(cand):
            return count_keys(lambda k, s0: jnp.where(k == thr, s0 + row, cand) < cand)

        cut = _tie_cut(ties_before, need, (1, tq), keys_t.shape[0] * ts)
        cut_s[...] = jnp.broadcast_to(cut, cut_s.shape)

    for g in range(N_KV_HEADS):
        for hh in range(GROUP):
            h = g * GROUP + hh
            q_r[g, hh * tq:(hh + 1) * tq, :] = q_ref[:, h * HEAD_DIM:(h + 1) * HEAD_DIM]
    acc[...] = jnp.zeros(acc.shape, F32)
    m_s[...] = jnp.full(m_s.shape, -jnp.inf, F32)
    ones = jnp.ones((ts, HEAD_DIM), BF16)
    n_lt = ts // LANES

    def attend_chunk(j, carry):
        cut = cut_s[0:1, :]
        for r in range(n_lt):
            k = keys_t[j, r * LANES:(r + 1) * LANES, :]
            pos = j * ts + r * LANES + lax.broadcasted_iota(I32, (LANES, tq), 0)
            tie_bias = jnp.where(pos <= cut, 0.0, NEG_BIG)
            mb = jnp.where(k > thr, 0.0, jnp.where(k == thr, tie_bias, NEG_BIG))
            for qt in range(n_qt):
                maskb[qt * LANES:(qt + 1) * LANES, r * LANES:(r + 1) * LANES] = mb[:, qt * LANES:(qt + 1) * LANES].T

        rel = (j * ts - q0 + lax.broadcasted_iota(I32, (1, ts), 1)).astype(F32)
        for g in range(N_KV_HEADS):
            s_all = jnp.dot(q_r[g], kt_ref[j, g * HEAD_DIM:(g + 1) * HEAD_DIM, :],
                            preferred_element_type=F32)
            v_aug = jnp.concatenate([v_ref[j, :, g * HEAD_DIM:(g + 1) * HEAD_DIM], ones], axis=1)
            for hh in range(GROUP):
                rows = slice(hh * tq, (hh + 1) * tq)
                lg = s_all[rows] + ((SLOPES[g * GROUP + hh] * LOG2E) * rel) + maskb[...]
                m_old = m_s[g, rows, :]
                m_new = jnp.maximum(m_old, jnp.max(lg, axis=1, keepdims=True))
                m_s[g, rows, :] = m_new
                alpha = jnp.exp2(m_old - m_new)
                p = jnp.exp2(lg - jnp.concatenate([m_new] * n_lt, axis=1)).astype(BF16)
                pv = jnp.dot(p, v_aug, preferred_element_type=F32)
                acc[g, rows, :] = acc[g, rows, :] * jnp.concatenate([alpha, alpha], axis=1) + pv
        return carry

    lax.fori_loop(0, nj, attend_chunk, 0)

    for h in range(N_HEADS):
        a = acc[h // GROUP, (h % GROUP) * tq:(h % GROUP + 1) * tq, :]
        o_ref[:, h * HEAD_DIM:(h + 1) * HEAD_DIM] = (a[:, :HEAD_DIM] / a[:, HEAD_DIM:]).astype(o_ref.dtype)


def attn_prompt(qiq, proj, col_iq, col_q, col_iw, ikt3, kt3, v3, tq=256):
    t = proj.shape[0]
    nc, _, ts = ikt3.shape
    kvw = kt3.shape[1]
    aw = N_HEADS * HEAD_DIM
    iqw = N_IDX_HEADS * IDX_DIM
    resident = dict(pipeline_mode=pl.Buffered(1))
    return pl.pallas_call(
        _attn_prompt_kernel,
        grid=(t // tq,),
        in_specs=[pl.BlockSpec((tq, iqw), lambda i: (i, col_iq)),
                  pl.BlockSpec((tq, aw), lambda i: (i, col_q)),
                  pl.BlockSpec((tq, LANES), lambda i: (i, col_iw)),
                  pl.BlockSpec((nc, IDX_DIM, ts), lambda i: (0, 0, 0), **resident),
                  pl.BlockSpec((nc, kvw, ts), lambda i: (0, 0, 0), **resident),
                  pl.BlockSpec((nc, ts, kvw), lambda i: (0, 0, 0), **resident)],
        out_specs=pl.BlockSpec((tq, aw), lambda i: (i, 0)),
        out_shape=jax.ShapeDtypeStruct((t, aw), BF16),
        scratch_shapes=[pltpu.VMEM((tq // IDX_TR, N_IDX_HEADS * IDX_TR, IDX_DIM), BF16),
                        pltpu.VMEM((tq // IDX_TR, N_IDX_HEADS * IDX_TR, LANES), F32),
                        pltpu.VMEM((tq, ts), I32),
                        pltpu.VMEM((nc, ts, tq), I32),
                        pltpu.VMEM((N_KV_HEADS, GROUP * tq, HEAD_DIM), BF16),
                        pltpu.VMEM((N_KV_HEADS, GROUP * tq, 2 * HEAD_DIM), F32),
                        pltpu.VMEM((N_KV_HEADS, GROUP * tq, LANES), F32),
                        pltpu.VMEM((tq, ts), F32),
                        pltpu.VMEM((8, tq), I32)],
        compiler_params=_cparams(("parallel",)),
        name="attn_prompt",
    )(qiq, qiq, proj, ikt3, kt3, v3)


NT_DIMS = (((1,), (1,)), ((), ()))
SAMPLE_SEARCH_BITS = 4


def _attn_sample_kernel(pt_ref, iq_ref, q_ref, iw_ref, kn_ref, vn_ref, ikn_ref, slope_ref, *refs):
    n_pages = (len(refs) - 2) // 3
    ik_refs, k_refs, v_refs = refs[:n_pages], refs[n_pages:2 * n_pages], refs[2 * n_pages:3 * n_pages]
    o_ref, lg_s = refs[3 * n_pages], refs[3 * n_pages + 1]
    t = q_ref.shape[1]
    past = n_pages * PAGE
    vk = PAGE * N_KV_HEADS
    rows = N_HEADS * t

    iq_r = jnp.concatenate([iq_ref[0, :, h * IDX_DIM:(h + 1) * IDX_DIM].astype(F32) for h in range(N_IDX_HEADS)],
                           axis=0).astype(BF16)
    wb = jnp.concatenate([jnp.broadcast_to(iw_ref[0, :, h:h + 1] * IDX_SCALE, (t, LANES))
                          for h in range(N_IDX_HEADS)], axis=0)

    def page_scores(ik_page):
        d = lax.dot_general(iq_r, ik_page, NT_DIMS, preferred_element_type=F32)
        sc = jnp.zeros((t, PAGE), F32)
        for h in range(N_IDX_HEADS):
            sc = sc + wb[h * t:(h + 1) * t, :] * jnp.maximum(d[h * t:(h + 1) * t, :], 0.0)
        return sc

    lane_t = lax.broadcasted_iota(I32, (t, PAGE), 1)
    row_t = lax.broadcasted_iota(I32, (t, PAGE), 0)
    keys = [_score_key(page_scores(ik_refs[p][0].astype(BF16)), lane_t >= 0) for p in range(n_pages)]
    ik_new = jnp.concatenate([ikn_ref[0], jnp.zeros((PAGE - t, IDX_DIM), F32)], axis=0)
    keys.append(_score_key(page_scores(ik_new.astype(BF16)), lane_t <= row_t))

    def count(pred):
        cnt = jnp.zeros((t, LANES), I32)
        for p, k in enumerate(keys):
            cnt = cnt + jnp.where(pred(k, p * PAGE + lane_t), 1, 0)
        return jnp.sum(cnt, axis=1, keepdims=True)

    thr = _kth_largest_key(lambda cand: count(lambda k, pos: k >= cand), (t, LANES), SAMPLE_SEARCH_BITS)
    need = INDEX_TOPK - count(lambda k, pos: k > thr)
    cut = _tie_cut(lambda cand: count(lambda k, pos: jnp.where(k == thr, pos, cand) < cand),
                   need, (t, LANES), past + PAGE, SAMPLE_SEARCH_BITS)

    def select_bias(k, pos):
        tie = jnp.where(pos <= cut, 0.0, NEG_BIG)
        return jnp.where(k > thr, 0.0, jnp.where(k == thr, tie, NEG_BIG)).astype(BF16)

    q_all = jnp.concatenate([q_ref[0, :, h * HEAD_DIM:(h + 1) * HEAD_DIM].astype(F32) for h in range(N_HEADS)],
                            axis=0).astype(BF16)
    slope = slope_ref[...]

    def tiled(x, n):
        return jnp.concatenate([x] * n, axis=1)

    r_i = lax.broadcasted_iota(I32, (rows, vk), 0)
    c_i = lax.broadcasted_iota(I32, (rows, vk), 1)
    head_ok = (r_i // (t * GROUP)) == (c_i % N_KV_HEADS)
    tok = lax.broadcasted_iota(I32, (PAGE, vk), 0)
    expand = jnp.where(tok == lax.broadcasted_iota(I32, (PAGE, vk), 1) // N_KV_HEADS, 1.0, 0.0).astype(BF16)
    rel0 = (lax.broadcasted_iota(I32, (1, vk), 1) // N_KV_HEADS - past).astype(F32)
    base = tiled(slope, vk // LANES) * rel0 + jnp.where(head_ok, 0.0, NEG_BIG)

    mx = jnp.full((rows, vk), -jnp.inf, F32)
    for p in range(n_pages):
        s = lax.dot_general(q_all, k_refs[p][0].astype(BF16), NT_DIMS, preferred_element_type=F32)
        sel = jnp.dot(select_bias(keys[p], p * PAGE + lane_t), expand, preferred_element_type=F32)
        lg = s + base + tiled(slope * float(p * PAGE), vk // LANES) \
            + jnp.concatenate([sel] * N_HEADS, axis=0)
        lg_s[p] = lg
        mx = jnp.maximum(mx, lg)

    def nk(ref):
        return jnp.concatenate([ref[0, :, g * HEAD_DIM:(g + 1) * HEAD_DIM] for g in range(N_KV_HEADS)]
                               + [jnp.zeros((PAGE - N_KV_HEADS * t, HEAD_DIM), F32)], axis=0).astype(BF16)

    rn =lax.broadcasted_iota(I32, (rows, PAGE), 0)
    cn = lax.broadcasted_iota(I32, (rows, PAGE), 1)
    head_ok_n = ((rn // (t * GROUP)) == (cn // t)) & (cn < N_KV_HEADS * t)
    tok_n = lax.broadcasted_iota(I32, (PAGE, PAGE), 0)
    col_n = lax.broadcasted_iota(I32, (PAGE, PAGE), 1)
    expand_n = jnp.where((tok_n == col_n % t) & (col_n < N_KV_HEADS * t), 1.0, 0.0).astype(BF16)
    rel_n = (lax.broadcasted_iota(I32, (1, PAGE), 1) % t).astype(F32)
    s_n = lax.dot_general(q_all, nk(kn_ref), NT_DIMS, preferred_element_type=F32)
    sel_n = jnp.dot(select_bias(keys[n_pages], past + lane_t), expand_n, preferred_element_type=F32)
    lg_n = s_n + slope * rel_n + jnp.where(head_ok_n, 0.0, NEG_BIG) \
        + jnp.concatenate([sel_n] * N_HEADS, axis=0)

    m_row = jnp.maximum(jnp.max(mx, axis=1, keepdims=True), jnp.max(lg_n, axis=1, keepdims=True))
    pr_n = jnp.exp2(lg_n - m_row)
    num = jnp.dot(pr_n.astype(BF16), nk(vn_ref), preferred_element_type=F32)
    den_n = jnp.sum(pr_n, axis=1, keepdims=True)
    den = jnp.zeros((rows, vk), F32)
    for p in range(n_pages):
        pr = jnp.exp2(lg_s[p] - m_row)
        den = den + pr
        num = num + jnp.dot(pr.astype(BF16), v_refs[p][0].astype(BF16), preferred_element_type=F32)
    out = num / (jnp.sum(den, axis=1, keepdims=True) + den_n)
    for h in range(N_HEADS):
        o_ref[0, :, h * HEAD_DIM:(h + 1) * HEAD_DIM] = out[h * t:(h + 1) * t, :]


def attn_sample(page_table, qiq3, proj3, cols, cache_k, cache_v, cache_ik):
    n, t, _ = proj3.shape
    n_pages = page_table.shape[1]
    kvw = N_KV_HEADS * HEAD_DIM
    aw = N_HEADS * HEAD_DIM
    iqw = N_IDX_HEADS * IDX_DIM
    vk = PAGE * N_KV_HEADS
    slope_rows = jnp.asarray(np.repeat(np.asarray(SLOPES, np.float64) * LOG2E, t)[:, None]
                             * np.ones((1, LANES)), F32)

    def pspec(width, col):
        return pl.BlockSpec((1, t, width), lambda b, pt: (b, 0, col))

    def cspec(rows, width, p):
        return pl.BlockSpec((1, rows, width), lambda b, pt: (pt[b, p], 0, 0))

    grid_spec = pltpu.PrefetchScalarGridSpec(
        num_scalar_prefetch=1,
        grid=(n,),
        in_specs=[pspec(iqw, cols["iq"]), pspec(aw, cols["q"]), pspec(LANES, cols["iw"]),
                  pspec(kvw, cols["k"]), pspec(kvw, cols["v"]), pspec(IDX_DIM, cols["ik"]),
                  pl.BlockSpec((N_HEADS * t, LANES), lambda b, pt: (0, 0))]
        + [cspec(PAGE, IDX_DIM, p) for p in range(n_pages)]
        + [cspec(vk, HEAD_DIM, p) for p in range(n_pages)]
        + [cspec(vk, HEAD_DIM, p) for p in range(n_pages)],
        out_specs=pl.BlockSpec((1, t, aw), lambda b, pt: (b, 0, 0)),
        scratch_shapes=[pltpu.VMEM((n_pages, N_HEADS * t, vk), F32)],
    )
    return pl.pallas_call(
        _attn_sample_kernel,
        grid_spec=grid_spec,
        out_shape=jax.ShapeDtypeStruct((n, t, aw), F32),
        compiler_params=_cparams(("parallel",)),
        name="attn_sample",
    )(page_table, qiq3, qiq3, proj3, proj3, proj3, proj3, slope_rows,
      *([cache_ik] * n_pages), *([cache_k] * n_pages), *([cache_v] * n_pages))


ATT_TS = 512


def _layer_weights(w_in, w_o, w_up, w_down, d_model):
    aw = N_HEADS * HEAD_DIM
    kvw = N_KV_HEADS * HEAD_DIM
    iqw = N_IDX_HEADS * IDX_DIM
    sizes = (aw, kvw, kvw, iqw, IDX_DIM, N_IDX_HEADS, w_in.shape[1] - (aw + 2 * kvw + iqw + IDX_DIM + N_IDX_HEADS))
    offs = [0]
    for s in sizes:
        offs.append(offs[-1] + s)
    wq, wk, wv, wiq, wik, wiw, wu = (w_in[:, offs[i]:offs[i + 1]] for i in range(7))
    wiw = jnp.pad(wiw, ((0, 0), (0, LANES - N_IDX_HEADS)))
    w_proj = jnp.concatenate([wiq, wq, wu, wk, wv, wik, wiw], axis=1).astype(BF16)
    cu = sizes[6]
    assert iqw % aw == 0 and cu % kvw == 0
    cols = {"iq": 0, "q": iqw // aw,
            "u": 0, "k": cu // kvw, "v": cu // kvw + 1, "ik": (cu + 2 * kvw) // IDX_DIM,
            "iw": (cu + 2 * kvw) // IDX_DIM + 1}
    col_scale = jnp.asarray(np.concatenate([np.ones(iqw), np.full(aw, ATT_SCALE * LOG2E)])[None, :], F32)
    return w_proj, col_scale, cols, w_o.astype(BF16), w_up.astype(BF16), w_down.astype(BF16)


def kernel(x_prompt, x_sample, cache_k, cache_v, cache_idx_k, state_conv, page_table, c_prompt, c_sample, w_ada, b_ada, g_mix, w_in, w_o, conv_w, conv_b, conv_ln_g, conv_ln_b, g_mlp, w_up, w_down, w_ada_final, b_ada_final, g_final):
    nb, seq, d = x_prompt.shape
    ns, ts_, _ = x_sample.shape
    depth = w_ada.shape[0]
    assert nb == 1 and depth == 1
    kvw = N_KV_HEADS * HEAD_DIM
    cw = conv_w.shape[2]
    k1 = CONV_K - 1

    n_c = ns + nb
    pad = (-n_c) % 16
    c_all = jnp.concatenate([c_sample, c_prompt, jnp.zeros((pad, d), F32)], axis=0)
    mods = ada_matmul(c_all, w_ada[0], b_ada[0])
    mods_f = ada_matmul(c_all, w_ada_final, b_ada_final)

    def mod_p(arr, idx):
        return arr[ns:ns + 1, idx * d:(idx + 1) * d]

    def mod_s(arr, idx):
        return jnp.repeat(arr[:ns, idx * d:(idx + 1) * d], ts_, axis=0)

    w_proj, col_scale, cols, w_o_b, w_up_b, w_down_b = _layer_weights(w_in[0], w_o[0], w_up[0], w_down[0], d)
    kcol = cols["k"] * kvw
    ikcol = cols["ik"] * IDX_DIM

    def trunk(x2, mod, attn, conv):
        x1 = out_proj_residual(attn, conv, w_o_b, x2, mod(mods, 2))
        h2 = norm_modulate(x1, g_mlp[0], mod(mods, 3), mod(mods, 4), BF16)
        m = mlp(h2, w_up_b, w_down_b)
        return residual_final_norm(x1, m, mod(mods, 5), g_final, mod(mods_f, 0), mod(mods_f, 1))

    xp = x_prompt.reshape(seq, d)
    hp = norm_modulate(xp, g_mix[0], mod_p(mods, 0), mod_p(mods, 1), BF16)
    qiq_p, proj_p = in_proj(hp, w_proj, col_scale)
    k_p = proj_p[:, kcol:kcol + kvw]
    v_p = proj_p[:, kcol + kvw:kcol + 2 * kvw]
    ik_p = proj_p[:, ikcol:ikcol + IDX_DIM]
    nc = seq // ATT_TS
    ikt3 = ik_p.astype(BF16).reshape(nc, ATT_TS, IDX_DIM).transpose(0, 2, 1)
    kt3 = k_p.astype(BF16).reshape(nc, ATT_TS, kvw).transpose(0, 2, 1)
    v3 = v_p.astype(BF16).reshape(nc, ATT_TS, kvw)
    attn_p = attn_prompt(qiq_p, proj_p, cols["iq"], cols["q"], cols["iw"], ikt3, kt3, v3)
    conv_p, cstate_p = conv_prompt(proj_p, conv_w[0], conv_b[0], conv_ln_g[0], conv_ln_b[0])
    y_p = trunk(xp, mod_p, attn_p, conv_p)

    xs = x_sample.reshape(ns * ts_, d)
    hs = norm_modulate(xs, g_mix[0], mod_s(mods, 0), mod_s(mods, 1), BF16)
    qiq_s, proj_s = in_proj(hs, w_proj, col_scale)
    proj_s3 = proj_s.reshape(ns, ts_, proj_s.shape[1])
    n_pool = cache_k.shape[1]
    attn_s = attn_sample(page_table, qiq_s.reshape(ns, ts_, qiq_s.shape[1]), proj_s3, cols,
                         cache_k[0].reshape(n_pool, PAGE * N_KV_HEADS, HEAD_DIM),
                         cache_v[0].reshape(n_pool, PAGE * N_KV_HEADS, HEAD_DIM), cache_idx_k[0])
    conv_s, cstate_s = conv_sample(proj_s3, state_conv[0], conv_w[0], conv_b[0], conv_ln_g[0], conv_ln_b[0])
    y_s = trunk(xs, mod_s, attn_s.reshape(ns * ts_, -1).astype(BF16), conv_s.reshape(ns * ts_, cw).astype(BF16))

    return (y_p.reshape(nb, seq, d),
            y_s.reshape(ns, ts_, d),
            k_p.reshape(depth, nb, seq, N_KV_HEADS, HEAD_DIM),
            v_p.reshape(depth, nb, seq, N_KV_HEADS, HEAD_DIM),
            ik_p.reshape(depth, nb, seq, IDX_DIM),
            cstate_p.reshape(depth, nb, k1, cw),
            proj_s[:, kcol:kcol + kvw].reshape(depth, ns, ts_, N_KV_HEADS, HEAD_DIM),
            proj_s[:, kcol + kvw:kcol + 2 * kvw].reshape(depth, ns, ts_, N_KV_HEADS, HEAD_DIM),
            proj_s[:, ikcol:ikcol + IDX_DIM].reshape(depth, ns, ts_, IDX_DIM),
            cstate_s.reshape(depth, ns, k1, cw))
```

```python
import functools

import jax
import jax.numpy as jnp
import numpy as np
from jax import lax
from jax.experimental import pallas as pl
from jax.experimental.pallas import tpu as pltpu

F32 = jnp.float32
BF16 = jnp.bfloat16
I32 = jnp.int32

HEAD_DIM = 128
N_HEADS = 16
N_KV_HEADS = 4
GROUP = N_HEADS // N_KV_HEADS
N_IDX_HEADS = 32
IDX_DIM = 128
INDEX_TOPK = 256
CONV_K = 31
EPS = 1e-6
PAGE = 128

LANES = 128
SUBLANES = 8
INT_MIN = -2 ** 31
NEG_BIG = -1e30
VMEM_LIMIT = 56 * 1024 * 1024

SLOPES = tuple(2.0 ** (-8.0 * (h + 1) / N_HEADS) for h in range(N_HEADS))
IDX_SCALE = (N_IDX_HEADS ** -0.5) * (IDX_DIM ** -0.5)
ATT_SCALE = HEAD_DIM ** -0.5


def _cparams(sem):
    return pltpu.CompilerParams(dimension_semantics=sem, vmem_limit_bytes=VMEM_LIMIT)


def _ada_kernel(c_ref, w_ref, b_ref, o_ref):
    c = c_ref[...]
    a = (c * jax.nn.sigmoid(c)).astype(BF16)
    o_ref[...] = jnp.dot(a, w_ref[...].astype(BF16), preferred_element_type=F32) + b_ref[...]


def ada_matmul(c, w, b, tn=512):
    m, d = c.shape
    n = w.shape[1]
    return pl.pallas_call(
        _ada_kernel,
        grid=(n // tn,),
        in_specs=[pl.BlockSpec((m, d), lambda j: (0, 0)),
                  pl.BlockSpec((d, tn), lambda j: (0, j)),
                  pl.BlockSpec((1, tn), lambda j: (0, j))],
        out_specs=pl.BlockSpec((m, tn), lambda j: (0, j)),
        out_shape=jax.ShapeDtypeStruct((m, n), F32),
        compiler_params=_cparams(("arbitrary",)),
        name="ada_matmul",
    )(c, w, b.reshape(1, n))


def _mod_spec(mod, tm, d):
    if mod.shape[0] == 1:
        return pl.BlockSpec((1, d), lambda i: (0, 0))
    return pl.BlockSpec((tm, d), lambda i: (i, 0))


def _norm_mod_kernel(x_ref, g_ref, sh_ref, sc_ref, o_ref):
    x = x_ref[...]
    y = x * lax.rsqrt(jnp.mean(x * x, axis=-1, keepdims=True) + EPS) * g_ref[...]
    o_ref[...] = (y * (1.0 + sc_ref[...]) + sh_ref[...]).astype(o_ref.dtype)


def norm_modulate(x, g, shift, scale, out_dtype, tm=256):
    m, d = x.shape
    tm = min(tm, m)
    return pl.pallas_call(
        _norm_mod_kernel,
        grid=(m // tm,),
        in_specs=[pl.BlockSpec((tm, d), lambda i: (i, 0)),
                  pl.BlockSpec((1, d), lambda i: (0, 0)),
                  _mod_spec(shift, tm, d), _mod_spec(scale, tm, d)],
        out_specs=pl.BlockSpec((tm, d), lambda i: (i, 0)),
        out_shape=jax.ShapeDtypeStruct((m, d), out_dtype),
        compiler_params=_cparams(("parallel",)),
        name="norm_modulate",
    )(x, g.reshape(1, d), shift, scale)


def _in_proj_kernel(a_ref, b_ref, s_ref, lo_ref, hi_ref, *, n_lo):
    j = pl.program_id(1)
    acc = jnp.dot(a_ref[...], b_ref[...], preferred_element_type=F32)

    @pl.when(j < n_lo)
    def _():
        lo_ref[...] = (acc * s_ref[...]).astype(lo_ref.dtype)

    @pl.when(j >= n_lo)
    def _():
        hi_ref[...] = acc


def in_proj(a, b, col_scale, tm=1024, tn=768):
    m, k = a.shape
    n = b.shape[1]
    n_bf16 = col_scale.shape[1]
    tm = min(tm, m)
    n_lo = n_bf16 // tn
    assert n_lo * tn == n_bf16 and n % tn == 0
    return pl.pallas_call(
        functools.partial(_in_proj_kernel, n_lo=n_lo),
        grid=(m // tm, n // tn),
        in_specs=[pl.BlockSpec((tm, k), lambda i, j: (i, 0)),
                  pl.BlockSpec((k, tn), lambda i, j: (0, j)),
                  pl.BlockSpec((1, tn), lambda i, j: (0, jnp.minimum(j, n_lo - 1)))],
        out_specs=[pl.BlockSpec((tm, tn), lambda i, j: (i, jnp.minimum(j, n_lo - 1))),
                   pl.BlockSpec((tm, tn), lambda i, j: (i, jnp.maximum(j - n_lo, 0)))],
        out_shape=[jax.ShapeDtypeStruct((m, n_bf16), BF16),
                   jax.ShapeDtypeStruct((m, n - n_bf16), F32)],
        compiler_params=_cparams(("parallel", "arbitrary")),
        name="in_proj",
    )(a, b, col_scale)


def _wo_kernel(a1_ref, a2_ref, w1_ref, w2_ref, x_ref, gt_ref, o_ref):
    acc = jnp.dot(a1_ref[...].astype(BF16), w1_ref[...], preferred_element_type=F32)
    acc += jnp.dot(a2_ref[...].astype(BF16), w2_ref[...], preferred_element_type=F32)
    o_ref[...] = x_ref[...] + gt_ref[...] * acc


def out_proj_residual(attn, conv, w_o, x, gate, tm=1024, tn=512):
    m, ka = attn.shape
    kc = conv.shape[1]
    assert ka == kc
    n = w_o.shape[1]
    tm = min(tm, m)
    if gate.shape[0] == 1:
        gspec = pl.BlockSpec((1, tn), lambda i, j: (0, j))
    else:
        gspec = pl.BlockSpec((tm, tn), lambda i, j: (i, j))
    return pl.pallas_call(
        _wo_kernel,
        grid=(m // tm, n // tn),
        in_specs=[pl.BlockSpec((tm, ka), lambda i, j: (i, 0)),
                  pl.BlockSpec((tm, kc), lambda i, j: (i, 0)),
                  pl.BlockSpec((ka, tn), lambda i, j: (0, j)),
                  pl.BlockSpec((kc, tn), lambda i, j: (1, j)),
                  pl.BlockSpec((tm, tn), lambda i, j: (i, j)),
                  gspec],
        out_specs=pl.BlockSpec((tm, tn), lambda i, j: (i, j)),
        out_shape=jax.ShapeDtypeStruct((m, n), F32),
        compiler_params=_cparams(("parallel", "arbitrary")),
        name="out_proj",
    )(attn, conv, w_o, w_o, x, gate)


def _mlp_kernel(h_ref, wu_ref, wd_ref, o_ref):
    @pl.when(pl.program_id(1) == 0)
    def _():
        o_ref[...] = jnp.zeros(o_ref.shape, F32)

    up = jnp.dot(h_ref[...], wu_ref[...], preferred_element_type=F32)
    act = jnp.square(jnp.maximum(up, 0.0)).astype(BF16)
    o_ref[...] += jnp.dot(act, wd_ref[...], preferred_element_type=F32)


def mlp(h, w_up, w_down, tm=1024, tf=512):
    m, d = h.shape
    ff = w_up.shape[1]
    tm = min(tm, m)
    return pl.pallas_call(
        _mlp_kernel,
        grid=(m // tm, ff // tf),
        in_specs=[pl.BlockSpec((tm, d), lambda i, f: (i, 0), pipeline_mode=pl.Buffered(1)),
                  pl.BlockSpec((d, tf), lambda i, f: (0, f)),
                  pl.BlockSpec((tf, d), lambda i, f: (f, 0))],
        out_specs=pl.BlockSpec((tm, d), lambda i, f: (i, 0), pipeline_mode=pl.Buffered(1)),
        out_shape=jax.ShapeDtypeStruct((m, d), F32),
        compiler_params=_cparams(("parallel", "arbitrary")),
        name="mlp",
    )(h, w_up, w_down)


def _final_kernel(x_ref, m_ref, gt_ref, g_ref, sh_ref, sc_ref, o_ref):
    x = x_ref[...] + gt_ref[...] * m_ref[...]
    y = x * lax.rsqrt(jnp.mean(x * x, axis=-1, keepdims=True) + EPS) * g_ref[...]
    o_ref[...] = y * (1.0 + sc_ref[...]) + sh_ref[...]


def residual_final_norm(x, mlp_out, gate, g, shift, scale, tm=256):
    m, d = x.shape
    tm = min(tm, m)
    return pl.pallas_call(
        _final_kernel,
        grid=(m // tm,),
        in_specs=[pl.BlockSpec((tm, d), lambda i: (i, 0)),
                  pl.BlockSpec((tm, d), lambda i: (i, 0)),
                  _mod_spec(gate, tm, d),
                  pl.BlockSpec((1, d), lambda i: (0, 0)),
                  _mod_spec(shift, tm, d), _mod_spec(scale, tm, d)],
        out_specs=pl.BlockSpec((tm, d), lambda i: (i, 0)),
        out_shape=jax.ShapeDtypeStruct((m, d), F32),
        compiler_params=_cparams(("parallel",)),
        name="final_norm",
    )(x, mlp_out, gate, g.reshape(1, d), shift, scale)


def _ln_silu(y, lg, lb):
    mu = jnp.mean(y, axis=-1, keepdims=True)
    yc = y - mu
    var = jnp.mean(yc * yc, axis=-1, keepdims=True)
    yn = yc * lax.rsqrt(var + EPS) * lg + lb
    return yn * jax.nn.sigmoid(yn)


CONV_HALO = 32
CONV_RT = 64
CONV_CT = 512


def _conv_prompt_kernel(u_ref, w_ref, b_ref, lg_ref, lb_ref, o_ref, st_ref, zbuf, ybuf, zs):
    tt = u_ref.shape[0]
    c = w_ref.shape[1]
    i = pl.program_id(0)

    @pl.when(i == 0)
    def _():
        zbuf[0:CONV_HALO, :] = jnp.zeros((CONV_HALO, c), F32)

    zbuf[CONV_HALO:CONV_HALO + tt, :] = u_ref[:, 0:c] * jax.nn.sigmoid(u_ref[:, c:2 * c])
    zs_rows = zs.shape[1]
    for r in range(1, SUBLANES):
        zs[r - 1] = zbuf[r:r + zs_rows, :]
    base = CONV_HALO - (CONV_K - 1)
    for rt in range(tt // CONV_RT):
        for ct in range(c // CONV_CT):
            cs = slice(ct * CONV_CT, (ct + 1) * CONV_CT)
            acc = jnp.zeros((CONV_RT, CONV_CT), F32)
            for j in range(CONV_K):
                a, r = divmod(base + j, SUBLANES)
                r0 = a * SUBLANES + rt * CONV_RT
                window = zbuf[r0:r0 + CONV_RT, cs] if r == 0 else zs[r - 1, r0:r0 + CONV_RT, cs]
                acc = acc + w_ref[j:j + 1, cs] * window
            ybuf[rt * CONV_RT:(rt + 1) * CONV_RT, cs] = acc + b_ref[:, cs]
    o_ref[...] = _ln_silu(ybuf[...], lg_ref[...], lb_ref[...]).astype(o_ref.dtype)
    st_ref[...] = zbuf[CONV_HALO + tt - (CONV_K - 1):CONV_HALO + tt, :]
    zbuf[0:CONV_HALO, :] = zbuf[tt:tt + CONV_HALO, :]


def conv_prompt(proj, conv_w, conv_b, ln_g, ln_b, tt=256):
    t = proj.shape[0]
    c = conv_w.shape[1]
    return pl.pallas_call(
        _conv_prompt_kernel,
        grid=(t // tt,),
        in_specs=[pl.BlockSpec((tt, 2 * c), lambda i: (i, 0)),
                  pl.BlockSpec((CONV_K, c), lambda i: (0, 0)),
                  pl.BlockSpec((1, c), lambda i: (0, 0)),
                  pl.BlockSpec((1, c), lambda i: (0, 0)),
                  pl.BlockSpec((1, c), lambda i: (0, 0))],
        out_specs=[pl.BlockSpec((tt, c), lambda i: (i, 0)),
                   pl.BlockSpec((CONV_K - 1, c), lambda i: (0, 0))],
        out_shape=[jax.ShapeDtypeStruct((t, c), BF16),
                   jax.ShapeDtypeStruct((CONV_K - 1, c), F32)],
        scratch_shapes=[pltpu.VMEM((CONV_HALO + tt, c), F32), pltpu.VMEM((tt, c), F32),
                        pltpu.VMEM((SUBLANES - 1, CONV_HALO + tt - SUBLANES, c), F32)],
        compiler_params=_cparams(("arbitrary",)),
        name="conv_prompt",
    )(proj, conv_w, conv_b.reshape(1, c), ln_g.reshape(1, c), ln_b.reshape(1, c))


def _conv_sample_kernel(u_ref, s_ref, w_ref, b_ref, lg_ref, lb_ref, o_ref, ns_ref, zp):
    bb, t, _ = u_ref.shape
    c = w_ref.shape[1]
    k1 = CONV_K - 1
    for b in range(bb):
        zp[0:k1, :] = s_ref[b]
        zp[k1:k1 + t, :] = u_ref[b, :, 0:c] * jax.nn.sigmoid(u_ref[b, :, c:2 * c])
        acc = jnp.zeros((t, c), F32)
        for j in range(CONV_K):
            acc = acc + w_ref[j:j + 1, :] * zp[j:j + t, :]
        o_ref[b] = _ln_silu(acc + b_ref[...], lg_ref[...], lb_ref[...])
        ns_ref[b] = zp[t:t + k1, :]


def conv_sample(proj3, state, conv_w, conv_b, ln_g, ln_b, bb=8):
    n, t, _ = proj3.shape
    c = conv_w.shape[1]
    k1 = CONV_K - 1
    return pl.pallas_call(
        _conv_sample_kernel,
        grid=(n // bb,),
        in_specs=[pl.BlockSpec((bb, t, 2 * c), lambda i: (i, 0, 0)),
                  pl.BlockSpec((bb, k1, c), lambda i: (i, 0, 0)),
                  pl.BlockSpec((CONV_K, c), lambda i: (0, 0)),
                  pl.BlockSpec((1, c), lambda i: (0, 0)),
                  pl.BlockSpec((1, c), lambda i: (0, 0)),
                  pl.BlockSpec((1, c), lambda i: (0, 0))],
        out_specs=[pl.BlockSpec((bb, t, c), lambda i: (i, 0, 0)),
                   pl.BlockSpec((bb, k1, c), lambda i: (i, 0, 0))],
        out_shape=[jax.ShapeDtypeStruct((n, t, c), F32),
                   jax.ShapeDtypeStruct((n, k1, c), F32)],
        scratch_shapes=[pltpu.VMEM((k1 + t + 2, c), F32)],
        compiler_params=_cparams(("parallel",)),
        name="conv_sample",
    )(proj3, state, conv_w, conv_b.reshape(1, c), ln_g.reshape(1, c), ln_b.reshape(1, c))


def _score_key(score, valid):
    bits = pltpu.bitcast(score, I32)
    key = bits ^ ((bits >> 31) & 0x7FFFFFFF)
    key = jnp.maximum(key, INT_MIN + 1)
    return jnp.where(valid, key, INT_MIN)


def _digit_search(ok, start, nbits, bits_per_step):
    assert nbits % bits_per_step == 0

    def step(i, cur):
        unit = jnp.left_shift(jnp.int32(1), nbits - bits_per_step * (i + 1))
        digit = jnp.zeros(start.shape, I32)
        for c in range(1, 2 ** bits_per_step):
            digit = digit + jnp.where(ok(cur + c * unit), 1, 0)
        return cur + digit * unit

    return lax.fori_loop(0, nbits // bits_per_step, step, start)


def _kth_largest_key(count_ge, shape, bits_per_step=1):
    thr = _digit_search(lambda cand: count_ge(cand) >= INDEX_TOPK, jnp.full(shape, INT_MIN, I32), 32, bits_per_step)
    return jnp.maximum(thr, INT_MIN + 1)


def _tie_cut(count_ties_before, need, shape, n_pos, bits_per_step=1):
    nbits = -(-n_pos.bit_length() // bits_per_step) * bits_per_step
    return _digit_search(lambda cand: count_ties_before(cand) < need, jnp.zeros(shape, I32), nbits, bits_per_step)


IDX_TR = 128
IDX_TN = 256
SEL_RB = 64
LOG2E = 1.4426950408889634
INT_MAX = 2 ** 31 - 1


def _attn_prompt_kernel(iq_ref, q_ref, iw_ref, ikt_ref, kt_ref, v_ref, o_ref,
                        iq_r, wb, ktile, keys_t, q_r, acc, m_s, maskb, cut_s):
    tq = q_ref.shape[0]
    ts = ikt_ref.shape[2]
    qb = pl.program_id(0)
    q0 = qb * tq
    nj = (q0 + tq - 1) // ts + 1
    n_sub = tq // IDX_TR
    n_qt = tq // LANES

    for s in range(n_sub):
        rs = slice(s * IDX_TR, (s + 1) * IDX_TR)
        for h in range(N_IDX_HEADS):
            hs = slice(h * IDX_TR, (h + 1) * IDX_TR)
            iq_r[s, hs, :] = iq_ref[rs, h * IDX_DIM:(h + 1) * IDX_DIM]
            wb[s, hs, :] = jnp.broadcast_to(iw_ref[rs, h:h + 1] * IDX_SCALE, (IDX_TR, LANES))

    def score_chunk(j, carry):
        for s in range(n_sub):
            t_pos = q0 + s * IDX_TR + lax.broadcasted_iota(I32, (IDX_TR, LANES), 0)
            for cn in range(ts // IDX_TN):
                d = jnp.dot(iq_r[s], ikt_ref[j, :, cn * IDX_TN:(cn + 1) * IDX_TN],
                            preferred_element_type=F32)
                for half in range(IDX_TN // LANES):
                    ls = slice(half * LANES, (half + 1) * LANES)
                    sc = jnp.zeros((IDX_TR, LANES), F32)
                    for h in range(N_IDX_HEADS):
                        hs = slice(h * IDX_TR, (h + 1) * IDX_TR)
                        sc = sc + wb[s, hs, :] * jnp.maximum(d[hs, ls], 0.0)
                    c0 = cn * IDX_TN + half * LANES
                    s_pos = j * ts + c0 + lax.broadcasted_iota(I32, (IDX_TR, LANES), 1)
                    ktile[s * IDX_TR:(s + 1) * IDX_TR, c0:c0 + LANES] = _score_key(sc, s_pos <= t_pos)
        for qt in range(n_qt):
            for cn in range(ts // LANES):
                keys_t[j, cn * LANES:(cn + 1) * LANES, qt * LANES:(qt + 1) * LANES] = \
                    ktile[qt * LANES:(qt + 1) * LANES, cn * LANES:(cn + 1) * LANES].T
        return carry

    lax.fori_loop(0, nj, score_chunk, 0)

    def count_keys(pred):
        def body(j, cnt):
            for r in range(ts // SEL_RB):
                k = keys_t[j, r * SEL_RB:(r + 1) * SEL_RB, :]
                cnt = cnt + jnp.where(pred(k, j * ts + r * SEL_RB), 1, 0)
            return cnt
        cnt = lax.fori_loop(0, nj, body, jnp.zeros((SEL_RB, tq), I32))
        return jnp.sum(cnt, axis=0, keepdims=True)

    thr = _kth_largest_key(lambda cand: count_keys(lambda k, s0: k >= cand), (1, tq))
    n_ge = count_keys(lambda k, s0: k >= thr)
    cut_s[...] = jnp.full(cut_s.shape, INT_MAX, I32)

    @pl.when(jnp.max(n_ge) > INDEX_TOPK)
    def _():
        row = lax.broadcasted_iota(I32, (SEL_RB, tq), 0)
        need = INDEX_TOPK - count_keys(lambda k, s0: k > thr)

        def ties_before(cand):
            return count_keys(lambda k, s0: jnp.where(k == thr, s0 + row, cand) < cand)

        cut = _tie_cut(ties_before, need, (1, tq), keys_t.shape[0] * ts)
        cut_s[...] = jnp.broadcast_to(cut, cut_s.shape)

    for g in range(N_KV_HEADS):
        for hh in range(GROUP):
            h = g * GROUP + hh
            q_r[g, hh * tq:(hh + 1) * tq, :] = q_ref[:, h * HEAD_DIM:(h + 1) * HEAD_DIM]
    acc[...] = jnp.zeros(acc.shape, F32)
    m_s[...] = jnp.full(m_s.shape, -jnp.inf, F32)
    ones = jnp.ones((ts, HEAD_DIM), BF16)
    n_lt = ts // LANES

    def attend_chunk(j, carry):
        cut = cut_s[0:1, :]
        for r in range(n_lt):
            k = keys_t[j, r * LANES:(r + 1) * LANES, :]
            pos = j * ts + r * LANES + lax.broadcasted_iota(I32, (LANES, tq), 0)
            tie_bias = jnp.where(pos <= cut, 0.0, NEG_BIG)
            mb = jnp.where(k > thr, 0.0, jnp.where(k == thr, tie_bias, NEG_BIG))
            for qt in range(n_qt):
                maskb[qt * LANES:(qt + 1) * LANES, r * LANES:(r + 1) * LANES] = mb[:, qt * LANES:(qt + 1) * LANES].T

        rel = (j * ts - q0 + lax.broadcasted_iota(I32, (1, ts), 1)).astype(F32)
        for g in range(N_KV_HEADS):
            s_all = jnp.dot(q_r[g], kt_ref[j, g * HEAD_DIM:(g + 1) * HEAD_DIM, :],
                            preferred_element_type=F32)
            v_aug = jnp.concatenate([v_ref[j, :, g * HEAD_DIM:(g + 1) * HEAD_DIM], ones], axis=1)
            for hh in range(GROUP):
                rows = slice(hh * tq, (hh + 1) * tq)
                lg = s_all[rows] + ((SLOPES[g * GROUP + hh] * LOG2E) * rel) + maskb[...]
                m_old = m_s[g, rows, :]
                m_new = jnp.maximum(m_old, jnp.max(lg, axis=1, keepdims=True))
                m_s[g, rows, :] = m_new
                alpha = jnp.exp2(m_old - m_new)
                p = jnp.exp2(lg - jnp.concatenate([m_new] * n_lt, axis=1)).astype(BF16)
                pv = jnp.dot(p, v_aug, preferred_element_type=F32)
                acc[g, rows, :] = acc[g, rows, :] * jnp.concatenate([alpha, alpha], axis=1) + pv
        return carry

    lax.fori_loop(0, nj, attend_chunk, 0)

    for h in range(N_HEADS):
        a = acc[h // GROUP, (h % GROUP) * tq:(h % GROUP + 1) * tq, :]
        o_ref[:, h * HEAD_DIM:(h + 1) * HEAD_DIM] = (a[:, :HEAD_DIM] / a[:, HEAD_DIM:]).astype(o_ref.dtype)


def attn_prompt(qiq, proj, col_iq, col_q, col_iw, ikt3, kt3, v3, tq=256):
    t = proj.shape[0]
    nc, _, ts = ikt3.shape
    kvw = kt3.shape[1]
    aw = N_HEADS * HEAD_DIM
    iqw = N_IDX_HEADS * IDX_DIM
    resident = dict(pipeline_mode=pl.Buffered(1))
    return pl.pallas_call(
        _attn_prompt_kernel,
        grid=(t // tq,),
        in_specs=[pl.BlockSpec((tq, iqw), lambda i: (i, col_iq)),
                  pl.BlockSpec((tq, aw), lambda i: (i, col_q)),
                  pl.BlockSpec((tq, LANES), lambda i: (i, col_iw)),
                  pl.BlockSpec((nc, IDX_DIM, ts), lambda i: (0, 0, 0), **resident),
                  pl.BlockSpec((nc, kvw, ts), lambda i: (0, 0, 0), **resident),
                  pl.BlockSpec((nc, ts, kvw), lambda i: (0, 0, 0), **resident)],
        out_specs=pl.BlockSpec((tq, aw), lambda i: (i, 0)),
        out_shape=jax.ShapeDtypeStruct((t, aw), BF16),
        scratch_shapes=[pltpu.VMEM((tq // IDX_TR, N_IDX_HEADS * IDX_TR, IDX_DIM), BF16),
                        pltpu.VMEM((tq // IDX_TR, N_IDX_HEADS * IDX_TR, LANES), F32),
                        pltpu.VMEM((tq, ts), I32),
                        pltpu.VMEM((nc, ts, tq), I32),
                        pltpu.VMEM((N_KV_HEADS, GROUP * tq, HEAD_DIM), BF16),
                        pltpu.VMEM((N_KV_HEADS, GROUP * tq, 2 * HEAD_DIM), F32),
                        pltpu.VMEM((N_KV_HEADS, GROUP * tq, LANES), F32),
                        pltpu.VMEM((tq, ts), F32),
                        pltpu.VMEM((8, tq), I32)],
        compiler_params=_cparams(("parallel",)),
        name="attn_prompt",
    )(qiq, qiq, proj, ikt3, kt3, v3)


NT_DIMS = (((1,), (1,)), ((), ()))
SAMPLE_SEARCH_BITS = 4
SAMPLE_BB = 2


def _attn_sample_kernel(pt_ref, iq_ref, q_ref, iw_ref, kn_ref, vn_ref, ikn_ref, slope_ref, *refs):
    bb, t = q_ref.shape[0], q_ref.shape[1]
    n_pages = (len(refs) - 3) // (3 * bb)
    o_ref, lg_s, cut_s = refs[3 * bb * n_pages:]
    past = n_pages * PAGE
    vk = PAGE * N_KV_HEADS
    rows = N_HEADS * t

    def page_refs(kind, b):
        start = (kind * bb + b) * n_pages
        return refs[start:start + n_pages]

    lane_t = lax.broadcasted_iota(I32, (t, PAGE), 1)
    row_t = lax.broadcasted_iota(I32, (t, PAGE), 0)

    def sequence_keys(b):
        iq_r = jnp.concatenate([iq_ref[b, :, h * IDX_DIM:(h + 1) * IDX_DIM].astype(F32)
                                for h in range(N_IDX_HEADS)], axis=0).astype(BF16)
        wb = jnp.concatenate([jnp.broadcast_to(iw_ref[b, :, h:h + 1] * IDX_SCALE, (t, LANES))
                              for h in range(N_IDX_HEADS)], axis=0)

        def page_scores(ik_page):
            d = lax.dot_general(iq_r, ik_page, NT_DIMS, preferred_element_type=F32)
            sc = jnp.zeros((t, PAGE), F32)
            for h in range(N_IDX_HEADS):
                sc = sc + wb[h * t:(h + 1) * t, :] * jnp.maximum(d[h * t:(h + 1) * t, :], 0.0)
            return sc

        ks = [_score_key(page_scores(r[0].astype(BF16)), lane_t >= 0) for r in page_refs(0, b)]
        ik_new = jnp.concatenate([ikn_ref[b], jnp.zeros((PAGE - t, IDX_DIM), F32)], axis=0)
        return ks + [_score_key(page_scores(ik_new.astype(BF16)), lane_t <= row_t)]

    seq_keys = [sequence_keys(b) for b in range(bb)]
    keys = [jnp.concatenate([seq_keys[b][p] for b in range(bb)], axis=0) for p in range(n_pages + 1)]
    lane_s = lax.broadcasted_iota(I32, (bb * t, PAGE), 1)

    def count(pred):
        cnt = jnp.zeros((bb * t, LANES), I32)
        for p, k in enumerate(keys):
            cnt = cnt + jnp.where(pred(k, p * PAGE + lane_s), 1, 0)
        return jnp.sum(cnt, axis=1, keepdims=True)

    thr_all = _kth_largest_key(lambda cand: count(lambda k, pos: k >= cand), (bb * t, LANES), SAMPLE_SEARCH_BITS)
    n_ge = count(lambda k, pos: k >= thr_all)
    cut_s[...] = jnp.full(cut_s.shape, INT_MAX, I32)

    @pl.when(jnp.max(n_ge) > INDEX_TOPK)
    def _():
        need = INDEX_TOPK - count(lambda k, pos: k > thr_all)
        cut_s[...] = _tie_cut(lambda cand: count(lambda k, pos: jnp.where(k == thr_all, pos, cand) < cand),
                              need, (bb * t, LANES), past + PAGE, SAMPLE_SEARCH_BITS)

    cut_all = cut_s[...]
    slope = slope_ref[...]

    def tiled(x, n):
        return jnp.concatenate([x] * n, axis=1)

    r_i = lax.broadcasted_iota(I32, (rows, vk), 0)
    c_i = lax.broadcasted_iota(I32, (rows, vk), 1)
    head_ok = (r_i // (t * GROUP)) == (c_i % N_KV_HEADS)
    tok = lax.broadcasted_iota(I32, (PAGE, vk), 0)
    expand = jnp.where(tok == lax.broadcasted_iota(I32, (PAGE, vk), 1) // N_KV_HEADS, 1.0, 0.0).astype(BF16)
    rel0 = (lax.broadcasted_iota(I32, (1, vk), 1) // N_KV_HEADS - past).astype(F32)
    base = tiled(slope, vk // LANES) * rel0 + jnp.where(head_ok, 0.0, NEG_BIG)

    rn = lax.broadcasted_iota(I32, (rows, PAGE), 0)
    cn = lax.broadcasted_iota(I32, (rows, PAGE), 1)
    head_ok_n = ((rn // (t * GROUP)) == (cn // t)) & (cn < N_KV_HEADS * t)
    tok_n = lax.broadcasted_iota(I32, (PAGE, PAGE), 0)
    col_n = lax.broadcasted_iota(I32, (PAGE, PAGE), 1)
    expand_n = jnp.where((tok_n == col_n % t) & (col_n < N_KV_HEADS * t), 1.0, 0.0).astype(BF16)
    base_n = slope * (lax.broadcasted_iota(I32, (1, PAGE), 1) % t).astype(F32) + jnp.where(head_ok_n, 0.0, NEG_BIG)

    def attend(b):
        thr, cut = thr_all[b * t:(b + 1) * t], cut_all[b * t:(b + 1) * t]
        k_refs, v_refs = page_refs(1, b), page_refs(2, b)

        def select_bias(k, pos):
            tie = jnp.where(pos <= cut, 0.0, NEG_BIG)
            return jnp.where(k > thr, 0.0, jnp.where(k == thr, tie, NEG_BIG)).astype(BF16)

        def new_rows(ref):
            return jnp.concatenate([ref[b, :, g * HEAD_DIM:(g + 1) * HEAD_DIM] for g in range(N_KV_HEADS)]
                                   + [jnp.zeros((PAGE - N_KV_HEADS * t, HEAD_DIM), F32)], axis=0).astype(BF16)

        q_all = jnp.concatenate([q_ref[b, :, h * HEAD_DIM:(h + 1) * HEAD_DIM].astype(F32)
                                 for h in range(N_HEADS)], axis=0).astype(BF16)
        mx = jnp.full((rows, vk), -jnp.inf, F32)
        for p in range(n_pages):
            s = lax.dot_general(q_all, k_refs[p][0].astype(BF16), NT_DIMS, preferred_element_type=F32)
            sel = jnp.dot(select_bias(seq_keys[b][p], p * PAGE + lane_t), expand, preferred_element_type=F32)
            lg = s + base + tiled(slope * float(p * PAGE), vk // LANES) \
                + jnp.concatenate([sel] * N_HEADS, axis=0)
            lg_s[b, p] = lg
            mx = jnp.maximum(mx, lg)
        s_n = lax.dot_general(q_all, new_rows(kn_ref), NT_DIMS, preferred_element_type=F32)
        sel_n = jnp.dot(select_bias(seq_keys[b][n_pages], past + lane_t), expand_n, preferred_element_type=F32)
        lg_n = s_n + base_n + jnp.concatenate([sel_n] * N_HEADS, axis=0)

        m_row = jnp.maximum(jnp.max(mx, axis=1, keepdims=True), jnp.max(lg_n, axis=1, keepdims=True))
        pr_n = jnp.exp2(lg_n - m_row)
        num = jnp.dot(pr_n.astype(BF16), new_rows(vn_ref), preferred_element_type=F32)
        den_n = jnp.sum(pr_n, axis=1, keepdims=True)
        den = jnp.zeros((rows, vk), F32)
        for p in range(n_pages):
            pr = jnp.exp2(lg_s[b, p] - m_row)
            den = den + pr
            num = num + jnp.dot(pr.astype(BF16), v_refs[p][0].astype(BF16), preferred_element_type=F32)
        out = num / (jnp.sum(den, axis=1, keepdims=True) + den_n)
        for h in range(N_HEADS):
            o_ref[b, :, h * HEAD_DIM:(h + 1) * HEAD_DIM] = out[h * t:(h + 1) * t, :]

    for b in range(bb):
        attend(b)


def attn_sample(page_table, qiq3, proj3, cols, cache_k, cache_v, cache_ik):
    n, t, _ = proj3.shape
    n_pages = page_table.shape[1]
    kvw = N_KV_HEADS * HEAD_DIM
    aw = N_HEADS * HEAD_DIM
    iqw = N_IDX_HEADS * IDX_DIM
    vk = PAGE * N_KV_HEADS
    slope_rows = jnp.asarray(np.repeat(np.asarray(SLOPES, np.float64) * LOG2E, t)[:, None]
                             * np.ones((1, LANES)), F32)

    bb = SAMPLE_BB
    assert n % bb == 0

    def pspec(width, col):
        return pl.BlockSpec((bb, t, width), lambda i, pt: (i, 0, col))

    def cspec(rows, width, b, p):
        return pl.BlockSpec((1, rows, width), lambda i, pt: (pt[i * bb + b, p], 0, 0))

    def page_specs(rows, width):
        return [cspec(rows, width, b, p) for b in range(bb) for p in range(n_pages)]

    grid_spec = pltpu.PrefetchScalarGridSpec(
        num_scalar_prefetch=1,
        grid=(n // bb,),
        in_specs=[pspec(iqw, cols["iq"]), pspec(aw, cols["q"]), pspec(LANES, cols["iw"]),
                  pspec(kvw, cols["k"]), pspec(kvw, cols["v"]), pspec(IDX_DIM, cols["ik"]),
                  pl.BlockSpec((N_HEADS * t, LANES), lambda i, pt: (0, 0))]
        + page_specs(PAGE, IDX_DIM) + page_specs(vk, HEAD_DIM) + page_specs(vk, HEAD_DIM),
        out_specs=pl.BlockSpec((bb, t, aw), lambda i, pt: (i, 0, 0)),
        scratch_shapes=[pltpu.VMEM((bb, n_pages, N_HEADS * t, vk), F32),
                        pltpu.VMEM((bb * t, LANES), I32)],
    )
    n_refs = bb * n_pages
    return pl.pallas_call(
        _attn_sample_kernel,
        grid_spec=grid_spec,
        out_shape=jax.ShapeDtypeStruct((n, t, aw), F32),
        compiler_params=_cparams(("parallel",)),
        name="attn_sample",
    )(page_table, qiq3, qiq3, proj3, proj3, proj3, proj3, slope_rows,
      *([cache_ik] * n_refs), *([cache_k] * n_refs), *([cache_v] * n_refs))


ATT_TS = 512


def _layer_weights(w_in, w_o, w_up, w_down, d_model):
    aw = N_HEADS * HEAD_DIM
    kvw = N_KV_HEADS * HEAD_DIM
    iqw = N_IDX_HEADS * IDX_DIM
    sizes = (aw, kvw, kvw, iqw, IDX_DIM, N_IDX_HEADS, w_in.shape[1] - (aw + 2 * kvw + iqw + IDX_DIM + N_IDX_HEADS))
    offs = [0]
    for s in sizes:
        offs.append(offs[-1] + s)
    wq, wk, wv, wiq, wik, wiw, wu = (w_in[:, offs[i]:offs[i + 1]] for i in range(7))
    wiw = jnp.pad(wiw, ((0, 0), (0, LANES - N_IDX_HEADS)))
    w_proj = jnp.concatenate([wiq, wq, wu, wk, wv, wik, wiw], axis=1).astype(BF16)
    cu = sizes[6]
    assert iqw % aw == 0 and cu % kvw == 0
    cols = {"iq": 0, "q": iqw // aw,
            "u": 0, "k": cu // kvw, "v": cu // kvw + 1, "ik": (cu + 2 * kvw) // IDX_DIM,
            "iw": (cu + 2 * kvw) // IDX_DIM + 1}
    col_scale = jnp.asarray(np.concatenate([np.ones(iqw), np.full(aw, ATT_SCALE * LOG2E)])[None, :], F32)
    return w_proj, col_scale, cols, w_o.astype(BF16), w_up.astype(BF16), w_down.astype(BF16)


def kernel(x_prompt, x_sample, cache_k, cache_v, cache_idx_k, state_conv, page_table, c_prompt, c_sample, w_ada, b_ada, g_mix, w_in, w_o, conv_w, conv_b, conv_ln_g, conv_ln_b, g_mlp, w_up, w_down, w_ada_final, b_ada_final, g_final):
    nb, seq, d = x_prompt.shape
    ns, ts_, _ = x_sample.shape
    depth = w_ada.shape[0]
    assert nb == 1 and depth == 1
    kvw = N_KV_HEADS * HEAD_DIM
    cw = conv_w.shape[2]
    k1 = CONV_K - 1

    n_c = ns + nb
    pad = (-n_c) % 16
    c_all = jnp.concatenate([c_sample, c_prompt, jnp.zeros((pad, d), F32)], axis=0)
    mods = ada_matmul(c_all, w_ada[0], b_ada[0])
    mods_f = ada_matmul(c_all, w_ada_final, b_ada_final)

    def mod_p(arr, idx):
        return arr[ns:ns + 1, idx * d:(idx + 1) * d]

    def mod_s(arr, idx):
        return jnp.repeat(arr[:ns, idx * d:(idx + 1) * d], ts_, axis=0)

    w_proj, col_scale, cols, w_o_b, w_up_b, w_down_b = _layer_weights(w_in[0], w_o[0], w_up[0], w_down[0], d)
    kcol = cols["k"] * kvw
    ikcol = cols["ik"] * IDX_DIM

    def trunk(x2, mod, attn, conv):
        x1 = out_proj_residual(attn, conv, w_o_b, x2, mod(mods, 2))
        h2 = norm_modulate(x1, g_mlp[0], mod(mods, 3), mod(mods, 4), BF16)
        m = mlp(h2, w_up_b, w_down_b)
        return residual_final_norm(x1, m, mod(mods, 5), g_final, mod(mods_f, 0), mod(mods_f, 1))

    xp = x_prompt.reshape(seq, d)
    hp = norm_modulate(xp, g_mix[0], mod_p(mods, 0), mod_p(mods, 1), BF16)
    qiq_p, proj_p = in_proj(hp, w_proj, col_scale)
    k_p = proj_p[:, kcol:kcol + kvw]
    v_p = proj_p[:, kcol + kvw:kcol + 2 * kvw]
    ik_p = proj_p[:, ikcol:ikcol + IDX_DIM]
    nc = seq // ATT_TS
    ikt3 = ik_p.astype(BF16).reshape(nc, ATT_TS, IDX_DIM).transpose(0, 2, 1)
    kt3 = k_p.astype(BF16).reshape(nc, ATT_TS, kvw).transpose(0, 2, 1)
    v3 = v_p.astype(BF16).reshape(nc, ATT_TS, kvw)
    attn_p = attn_prompt(qiq_p, proj_p, cols["iq"], cols["q"], cols["iw"], ikt3, kt3, v3)
    conv_p, cstate_p = conv_prompt(proj_p, conv_w[0], conv_b[0], conv_ln_g[0], conv_ln_b[0])
    y_p = trunk(xp, mod_p, attn_p, conv_p)

    xs = x_sample.reshape(ns * ts_, d)
    hs = norm_modulate(xs, g_mix[0], mod_s(mods, 0), mod_s(mods, 1), BF16)
    qiq_s, proj_s = in_proj(hs, w_proj, col_scale)
    proj_s3 = proj_s.reshape(ns, ts_, proj_s.shape[1])
    n_pool = cache_k.shape[1]
    attn_s = attn_sample(page_table, qiq_s.reshape(ns, ts_, qiq_s.shape[1]), proj_s3, cols,
                         cache_k[0].reshape(n_pool, PAGE * N_KV_HEADS, HEAD_DIM),
                         cache_v[0].reshape(n_pool, PAGE * N_KV_HEADS, HEAD_DIM), cache_idx_k[0])
    conv_s, cstate_s = conv_sample(proj_s3, state_conv[0], conv_w[0], conv_b[0], conv_ln_g[0], conv_ln_b[0])
    y_s = trunk(xs, mod_s, attn_s.reshape(ns * ts_, -1).astype(BF16), conv_s.reshape(ns * ts_, cw).astype(BF16))

    return (y_p.reshape(nb, seq, d),
            y_s.reshape(ns, ts_, d),
            k_p.reshape(depth, nb, seq, N_KV_HEADS, HEAD_DIM),
            v_p.reshape(depth, nb, seq, N_KV_HEADS, HEAD_DIM),
            ik_p.reshape(depth, nb, seq, IDX_DIM),
            cstate_p.reshape(depth, nb, k1, cw),
            proj_s[:, kcol:kcol + kvw].reshape(depth, ns, ts_, N_KV_HEADS, HEAD_DIM),
            proj_s[:, kcol + kvw:kcol + 2 * kvw].reshape(depth, ns, ts_, N_KV_HEADS, HEAD_DIM),
            proj_s[:, ikcol:ikcol + IDX_DIM].reshape(depth, ns, ts_, IDX_DIM),
            cstate_s.reshape(depth, ns, k1, cw))
```

```python
import functools

import jax
import jax.numpy as jnp
import numpy as np
from jax import lax
from jax.experimental import pallas as pl
from jax.experimental.pallas import tpu as pltpu

F32 = jnp.float32
BF16 = jnp.bfloat16
I32 = jnp.int32

HEAD_DIM = 128
N_HEADS = 16
N_KV_HEADS = 4
GROUP = N_HEADS // N_KV_HEADS
N_IDX_HEADS = 32
IDX_DIM = 128
INDEX_TOPK = 256
CONV_K = 31
EPS = 1e-6
PAGE = 128

LANES = 128
SUBLANES = 8
INT_MIN = -2 ** 31
NEG_BIG = -1e30
VMEM_LIMIT = 56 * 1024 * 1024

SLOPES = tuple(2.0 ** (-8.0 * (h + 1) / N_HEADS) for h in range(N_HEADS))
IDX_SCALE = (N_IDX_HEADS ** -0.5) * (IDX_DIM ** -0.5)
ATT_SCALE = HEAD_DIM ** -0.5


def _cparams(sem):
    return pltpu.CompilerParams(dimension_semantics=sem, vmem_limit_bytes=VMEM_LIMIT)


def _ada_kernel(c_ref, w_ref, b_ref, o_ref):
    c = c_ref[...]
    a = (c * jax.nn.sigmoid(c)).astype(BF16)
    o_ref[...] = jnp.dot(a, w_ref[...].astype(BF16), preferred_element_type=F32) + b_ref[...]


def ada_matmul(c, w, b, tn=512):
    m, d = c.shape
    n = w.shape[1]
    return pl.pallas_call(
        _ada_kernel,
        grid=(n // tn,),
        in_specs=[pl.BlockSpec((m, d), lambda j: (0, 0)),
                  pl.BlockSpec((d, tn), lambda j: (0, j)),
                  pl.BlockSpec((1, tn), lambda j: (0, j))],
        out_specs=pl.BlockSpec((m, tn), lambda j: (0, j)),
        out_shape=jax.ShapeDtypeStruct((m, n), F32),
        compiler_params=_cparams(("arbitrary",)),
        name="ada_matmul",
    )(c, w, b.reshape(1, n))


def _mod_spec(mod, tm, d):
    if mod.shape[0] == 1:
        return pl.BlockSpec((1, d), lambda i: (0, 0))
    return pl.BlockSpec((tm, d), lambda i: (i, 0))


def _norm_mod_kernel(x_ref, g_ref, sh_ref, sc_ref, o_ref):
    x = x_ref[...]
    y = x * lax.rsqrt(jnp.mean(x * x, axis=-1, keepdims=True) + EPS) * g_ref[...]
    o_ref[...] = (y * (1.0 + sc_ref[...]) + sh_ref[...]).astype(o_ref.dtype)


def norm_modulate(x, g, shift, scale, out_dtype, tm=256):
    m, d = x.shape
    tm = min(tm, m)
    return pl.pallas_call(
        _norm_mod_kernel,
        grid=(m // tm,),
        in_specs=[pl.BlockSpec((tm, d), lambda i: (i, 0)),
                  pl.BlockSpec((1, d), lambda i: (0, 0)),
                  _mod_spec(shift, tm, d), _mod_spec(scale, tm, d)],
        out_specs=pl.BlockSpec((tm, d), lambda i: (i, 0)),
        out_shape=jax.ShapeDtypeStruct((m, d), out_dtype),
        compiler_params=_cparams(("parallel",)),
        name="norm_modulate",
    )(x, g.reshape(1, d), shift, scale)


def _in_proj_kernel(a_ref, b_ref, s_ref, lo_ref, hi_ref):
    acc = jnp.dot(a_ref[...], b_ref[...], preferred_element_type=F32)
    hi_ref[...] = acc
    lo_ref[...] = (acc * s_ref[...]).astype(lo_ref.dtype)


def in_proj(a, b, col_scale, tm=1024, tn=768):
    m, k = a.shape
    n = b.shape[1]
    n_bf16 = col_scale.shape[1]
    tm = min(tm, m)
    n_hi = (n - n_bf16) // tn
    assert n_hi * tn == n - n_bf16 and n % tn == 0
    return pl.pallas_call(
        _in_proj_kernel,
        grid=(m // tm, n // tn),
        in_specs=[pl.BlockSpec((tm, k), lambda i, j: (i, 0)),
                  pl.BlockSpec((k, tn), lambda i, j: (0, j)),
                  pl.BlockSpec((1, tn), lambda i, j: (0, jnp.maximum(j - n_hi, 0)))],
        out_specs=[pl.BlockSpec((tm, tn), lambda i, j: (i, jnp.maximum(j - n_hi, 0))),
                   pl.BlockSpec((tm, tn), lambda i, j: (i, jnp.minimum(j, n_hi)))],
        out_shape=[jax.ShapeDtypeStruct((m, n_bf16), BF16),
                   jax.ShapeDtypeStruct((m, n - n_bf16 + tn), F32)],
        compiler_params=_cparams(("parallel", "arbitrary")),
        name="in_proj",
    )(a, b, col_scale)


def _wo_kernel(a1_ref, a2_ref, w1_ref, w2_ref, x_ref, gt_ref, o_ref):
    acc = jnp.dot(a1_ref[...].astype(BF16), w1_ref[...], preferred_element_type=F32)
    acc += jnp.dot(a2_ref[...].astype(BF16), w2_ref[...], preferred_element_type=F32)
    o_ref[...] = x_ref[...] + gt_ref[...] * acc


def out_proj_residual(attn, conv, w_o, x, gate, tm=1024, tn=512):
    m, ka = attn.shape
    kc = conv.shape[1]
    assert ka == kc
    n = w_o.shape[1]
    tm = min(tm, m)
    if gate.shape[0] == 1:
        gspec = pl.BlockSpec((1, tn), lambda i, j: (0, j))
    else:
        gspec = pl.BlockSpec((tm, tn), lambda i, j: (i, j))
    return pl.pallas_call(
        _wo_kernel,
        grid=(m // tm, n // tn),
        in_specs=[pl.BlockSpec((tm, ka), lambda i, j: (i, 0)),
                  pl.BlockSpec((tm, kc), lambda i, j: (i, 0)),
                  pl.BlockSpec((ka, tn), lambda i, j: (0, j)),
                  pl.BlockSpec((kc, tn), lambda i, j: (1, j)),
                  pl.BlockSpec((tm, tn), lambda i, j: (i, j)),
                  gspec],
        out_specs=pl.BlockSpec((tm, tn), lambda i, j: (i, j)),
        out_shape=jax.ShapeDtypeStruct((m, n), F32),
        compiler_params=_cparams(("parallel", "arbitrary")),
        name="out_proj",
    )(attn, conv, w_o, w_o, x, gate)


def _mlp_kernel(h_ref, wu_ref, wd_ref, o_ref):
    @pl.when(pl.program_id(1) == 0)
    def _():
        o_ref[...] = jnp.zeros(o_ref.shape, F32)

    up = jnp.dot(h_ref[...], wu_ref[...], preferred_element_type=F32)
    act = jnp.square(jnp.maximum(up, 0.0)).astype(BF16)
    o_ref[...] += jnp.dot(act, wd_ref[...], preferred_element_type=F32)


def mlp(h, w_up, w_down, tm=1024, tf=512):
    m, d = h.shape
    ff = w_up.shape[1]
    tm = min(tm, m)
    return pl.pallas_call(
        _mlp_kernel,
        grid=(m // tm, ff // tf),
        in_specs=[pl.BlockSpec((tm, d), lambda i, f: (i, 0), pipeline_mode=pl.Buffered(1)),
                  pl.BlockSpec((d, tf), lambda i, f: (0, f)),
                  pl.BlockSpec((tf, d), lambda i, f: (f, 0))],
        out_specs=pl.BlockSpec((tm, d), lambda i, f: (i, 0), pipeline_mode=pl.Buffered(1)),
        out_shape=jax.ShapeDtypeStruct((m, d), F32),
        compiler_params=_cparams(("parallel", "arbitrary")),
        name="mlp",
    )(h, w_up, w_down)


def _final_kernel(x_ref, m_ref, gt_ref, g_ref, sh_ref, sc_ref, o_ref):
    x = x_ref[...] + gt_ref[...] * m_ref[...]
    y = x * lax.rsqrt(jnp.mean(x * x, axis=-1, keepdims=True) + EPS) * g_ref[...]
    o_ref[...] = y * (1.0 + sc_ref[...]) + sh_ref[...]


def residual_final_norm(x, mlp_out, gate, g, shift, scale, tm=256):
    m, d = x.shape
    tm = min(tm, m)
    return pl.pallas_call(
        _final_kernel,
        grid=(m // tm,),
        in_specs=[pl.BlockSpec((tm, d), lambda i: (i, 0)),
                  pl.BlockSpec((tm, d), lambda i: (i, 0)),
                  _mod_spec(gate, tm, d),
                  pl.BlockSpec((1, d), lambda i: (0, 0)),
                  _mod_spec(shift, tm, d), _mod_spec(scale, tm, d)],
        out_specs=pl.BlockSpec((tm, d), lambda i: (i, 0)),
        out_shape=jax.ShapeDtypeStruct((m, d), F32),
        compiler_params=_cparams(("parallel",)),
        name="final_norm",
    )(x, mlp_out, gate, g.reshape(1, d), shift, scale)


def _ln_silu(y, lg, lb):
    mu = jnp.mean(y, axis=-1, keepdims=True)
    yc = y - mu
    var = jnp.mean(yc * yc, axis=-1, keepdims=True)
    yn = yc * lax.rsqrt(var + EPS) * lg + lb
    return yn * jax.nn.sigmoid(yn)


CONV_HALO = 32
CONV_RT = 64
CONV_CT = 512


def _conv_prompt_kernel(u_ref, w_ref, b_ref, lg_ref, lb_ref, o_ref, st_ref, zbuf, ybuf, zs):
    tt = u_ref.shape[0]
    c = w_ref.shape[1]
    i = pl.program_id(0)

    @pl.when(i == 0)
    def _():
        zbuf[0:CONV_HALO, :] = jnp.zeros((CONV_HALO, c), F32)

    zbuf[CONV_HALO:CONV_HALO + tt, :] = u_ref[:, 0:c] * jax.nn.sigmoid(u_ref[:, c:2 * c])
    zs_rows = zs.shape[1]
    for r in range(1, SUBLANES):
        zs[r - 1] = zbuf[r:r + zs_rows, :]
    base = CONV_HALO - (CONV_K - 1)
    for rt in range(tt // CONV_RT):
        for ct in range(c // CONV_CT):
            cs = slice(ct * CONV_CT, (ct + 1) * CONV_CT)
            acc = jnp.zeros((CONV_RT, CONV_CT), F32)
            for j in range(CONV_K):
                a, r = divmod(base + j, SUBLANES)
                r0 = a * SUBLANES + rt * CONV_RT
                window = zbuf[r0:r0 + CONV_RT, cs] if r == 0 else zs[r - 1, r0:r0 + CONV_RT, cs]
                acc = acc + w_ref[j:j + 1, cs] * window
            ybuf[rt * CONV_RT:(rt + 1) * CONV_RT, cs] = acc + b_ref[:, cs]
    o_ref[...] = _ln_silu(ybuf[...], lg_ref[...], lb_ref[...]).astype(o_ref.dtype)
    st_ref[...] = zbuf[CONV_HALO + tt - (CONV_K - 1):CONV_HALO + tt, :]
    zbuf[0:CONV_HALO, :] = zbuf[tt:tt + CONV_HALO, :]


def conv_prompt(proj, conv_w, conv_b, ln_g, ln_b, tt=256):
    t = proj.shape[0]
    c = conv_w.shape[1]
    return pl.pallas_call(
        _conv_prompt_kernel,
        grid=(t // tt,),
        in_specs=[pl.BlockSpec((tt, 2 * c), lambda i: (i, 0)),
                  pl.BlockSpec((CONV_K, c), lambda i: (0, 0)),
                  pl.BlockSpec((1, c), lambda i: (0, 0)),
                  pl.BlockSpec((1, c), lambda i: (0, 0)),
                  pl.BlockSpec((1, c), lambda i: (0, 0))],
        out_specs=[pl.BlockSpec((tt, c), lambda i: (i, 0)),
                   pl.BlockSpec((CONV_K - 1, c), lambda i: (0, 0))],
        out_shape=[jax.ShapeDtypeStruct((t, c), BF16),
                   jax.ShapeDtypeStruct((CONV_K - 1, c), F32)],
        scratch_shapes=[pltpu.VMEM((CONV_HALO + tt, c), F32), pltpu.VMEM((tt, c), F32),
                        pltpu.VMEM((SUBLANES - 1, CONV_HALO + tt - SUBLANES, c), F32)],
        compiler_params=_cparams(("arbitrary",)),
        name="conv_prompt",
    )(proj, conv_w, conv_b.reshape(1, c), ln_g.reshape(1, c), ln_b.reshape(1, c))


def _conv_sample_kernel(u_ref, s_ref, w_ref, b_ref, lg_ref, lb_ref, o_ref, ns_ref, zp):
    bb, t, _ = u_ref.shape
    c = w_ref.shape[1]
    k1 = CONV_K - 1
    for b in range(bb):
        zp[0:k1, :] = s_ref[b]
        zp[k1:k1 + t, :] = u_ref[b, :, 0:c] * jax.nn.sigmoid(u_ref[b, :, c:2 * c])
        acc = jnp.zeros((t, c), F32)
        for j in range(CONV_K):
            acc = acc + w_ref[j:j + 1, :] * zp[j:j + t, :]
        o_ref[b] = _ln_silu(acc + b_ref[...], lg_ref[...], lb_ref[...])
        ns_ref[b] = zp[t:t + k1, :]


def conv_sample(proj3, state, conv_w, conv_b, ln_g, ln_b, bb=8):
    n, t, _ = proj3.shape
    c = conv_w.shape[1]
    k1 = CONV_K - 1
    return pl.pallas_call(
        _conv_sample_kernel,
        grid=(n // bb,),
        in_specs=[pl.BlockSpec((bb, t, 2 * c), lambda i: (i, 0, 0)),
                  pl.BlockSpec((bb, k1, c), lambda i: (i, 0, 0)),
                  pl.BlockSpec((CONV_K, c), lambda i: (0, 0)),
                  pl.BlockSpec((1, c), lambda i: (0, 0)),
                  pl.BlockSpec((1, c), lambda i: (0, 0)),
                  pl.BlockSpec((1, c), lambda i: (0, 0))],
        out_specs=[pl.BlockSpec((bb, t, c), lambda i: (i, 0, 0)),
                   pl.BlockSpec((bb, k1, c), lambda i: (i, 0, 0))],
        out_shape=[jax.ShapeDtypeStruct((n, t, c), F32),
                   jax.ShapeDtypeStruct((n, k1, c), F32)],
        scratch_shapes=[pltpu.VMEM((k1 + t + 2, c), F32)],
        compiler_params=_cparams(("parallel",)),
        name="conv_sample",
    )(proj3, state, conv_w, conv_b.reshape(1, c), ln_g.reshape(1, c), ln_b.reshape(1, c))


def _score_key(score, valid):
    bits = pltpu.bitcast(score, I32)
    key = bits ^ ((bits >> 31) & 0x7FFFFFFF)
    key = jnp.maximum(key, INT_MIN + 1)
    return jnp.where(valid, key, INT_MIN)


def _digit_search(ok, start, nbits, bits_per_step):
    assert nbits % bits_per_step == 0

    def step(i, cur):
        unit = jnp.left_shift(jnp.int32(1), nbits - bits_per_step * (i + 1))
        digit = jnp.zeros(start.shape, I32)
        for c in range(1, 2 ** bits_per_step):
            digit = digit + jnp.where(ok(cur + c * unit), 1, 0)
        return cur + digit * unit

    return lax.fori_loop(0, nbits // bits_per_step, step, start)


def _kth_largest_key(count_ge, shape, bits_per_step=1):
    thr = _digit_search(lambda cand: count_ge(cand) >= INDEX_TOPK, jnp.full(shape, INT_MIN, I32), 32, bits_per_step)
    return jnp.maximum(thr, INT_MIN + 1)


def _tie_cut(count_ties_before, need, shape, n_pos, bits_per_step=1):
    nbits = -(-n_pos.bit_length() // bits_per_step) * bits_per_step
    return _digit_search(lambda cand: count_ties_before(cand) < need, jnp.zeros(shape, I32), nbits, bits_per_step)


IDX_TR = 128
IDX_TN = 256
SEL_RB = 64
LOG2E = 1.4426950408889634
INT_MAX = 2 ** 31 - 1
PV_HEADS = 1


def _attn_prompt_kernel(iq_ref, q_ref, iw_ref, ikt_ref, kt_ref, v_ref, o_ref,
                        iq_r, wb, ktile, keys_t, q_r, acc, m_s, maskb, cut_s):
    tq = q_ref.shape[0]
    ts = ikt_ref.shape[2]
    qb = pl.program_id(0)
    q0 = qb * tq
    nj = (q0 + tq - 1) // ts + 1
    n_sub = tq // IDX_TR
    n_qt = tq // LANES

    for s in range(n_sub):
        rs = slice(s * IDX_TR, (s + 1) * IDX_TR)
        for h in range(N_IDX_HEADS):
            hs = slice(h * IDX_TR, (h + 1) * IDX_TR)
            iq_r[s, hs, :] = iq_ref[rs, h * IDX_DIM:(h + 1) * IDX_DIM]
            wb[s, hs, :] = jnp.broadcast_to(iw_ref[rs, h:h + 1] * IDX_SCALE, (IDX_TR, LANES))

    def score_chunk(j, carry):
        for s in range(n_sub):
            t_pos = q0 + s * IDX_TR + lax.broadcasted_iota(I32, (IDX_TR, LANES), 0)
            for cn in range(ts // IDX_TN):
                d = jnp.dot(iq_r[s], ikt_ref[j, :, cn * IDX_TN:(cn + 1) * IDX_TN],
                            preferred_element_type=F32)
                for half in range(IDX_TN // LANES):
                    ls = slice(half * LANES, (half + 1) * LANES)
                    sc = jnp.zeros((IDX_TR, LANES), F32)
                    for h in range(N_IDX_HEADS):
                        hs = slice(h * IDX_TR, (h + 1) * IDX_TR)
                        sc = sc + wb[s, hs, :] * jnp.maximum(d[hs, ls], 0.0)
                    c0 = cn * IDX_TN + half * LANES
                    s_pos = j * ts + c0 + lax.broadcasted_iota(I32, (IDX_TR, LANES), 1)
                    ktile[s * IDX_TR:(s + 1) * IDX_TR, c0:c0 + LANES] = _score_key(sc, s_pos <= t_pos)
        for qt in range(n_qt):
            for cn in range(ts // LANES):
                keys_t[j, cn * LANES:(cn + 1) * LANES, qt * LANES:(qt + 1) * LANES] = \
                    ktile[qt * LANES:(qt + 1) * LANES, cn * LANES:(cn + 1) * LANES].T
        return carry

    lax.fori_loop(0, nj, score_chunk, 0)

    def count_keys(pred):
        def body(j, cnt):
            for r in range(ts // SEL_RB):
                k = keys_t[j, r * SEL_RB:(r + 1) * SEL_RB, :]
                cnt = cnt + jnp.where(pred(k, j * ts + r * SEL_RB), 1, 0)
            return cnt
        cnt = lax.fori_loop(0, nj, body, jnp.zeros((SEL_RB, tq), I32))
        return jnp.sum(cnt, axis=0, keepdims=True)

    thr = _kth_largest_key(lambda cand: count_keys(lambda k, s0: k >= cand), (1, tq))
    n_ge = count_keys(lambda k, s0: k >= thr)
    cut_s[...] = jnp.full(cut_s.shape, INT_MAX, I32)

    @pl.when(jnp.max(n_ge) > INDEX_TOPK)
    def _():
        row = lax.broadcasted_iota(I32, (SEL_RB, tq), 0)
        need = INDEX_TOPK - count_keys(lambda k, s0: k > thr)

        def ties_before(cand):
            return count_keys(lambda k, s0: jnp.where(k == thr, s0 + row, cand) < cand)

        cut = _tie_cut(ties_before, need, (1, tq), keys_t.shape[0] * ts)
        cut_s[...] = jnp.broadcast_to(cut, cut_s.shape)

    for g in range(N_KV_HEADS):
        for hh in range(GROUP):
            h = g * GROUP + hh
            q_r[g, hh * tq:(hh + 1) * tq, :] = q_ref[:, h * HEAD_DIM:(h + 1) * HEAD_DIM]
    acc[...] = jnp.zeros(acc.shape, F32)
    m_s[...] = jnp.full(m_s.shape, -jnp.inf, F32)
    ones = jnp.ones((ts, HEAD_DIM), BF16)
    n_lt = ts // LANES

    def attend_chunk(j, carry):
        cut = cut_s[0:1, :]
        for r in range(n_lt):
            k = keys_t[j, r * LANES:(r + 1) * LANES, :]
            pos = j * ts + r * LANES + lax.broadcasted_iota(I32, (LANES, tq), 0)
            tie_bias = jnp.where(pos <= cut, 0.0, NEG_BIG)
            mb = jnp.where(k > thr, 0.0, jnp.where(k == thr, tie_bias, NEG_BIG))
            for qt in range(n_qt):
                maskb[qt * LANES:(qt + 1) * LANES, r * LANES:(r + 1) * LANES] = mb[:, qt * LANES:(qt + 1) * LANES].T

        rel = (j * ts - q0 + lax.broadcasted_iota(I32, (1, ts), 1)).astype(F32)
        for g in range(N_KV_HEADS):
            s_all = jnp.dot(q_r[g], kt_ref[j, g * HEAD_DIM:(g + 1) * HEAD_DIM, :],
                            preferred_element_type=F32)
            v_aug = jnp.concatenate([v_ref[j, :, g * HEAD_DIM:(g + 1) * HEAD_DIM], ones], axis=1)
            ps, alphas = [], []
            for hh in range(GROUP):
                rows = slice(hh * tq, (hh + 1) * tq)
                lg = s_all[rows] + ((SLOPES[g * GROUP + hh] * LOG2E) * rel) + maskb[...]
                m_old = m_s[g, rows, :]
                m_new = jnp.maximum(m_old, jnp.max(lg, axis=1, keepdims=True))
                m_s[g, rows, :] = m_new
                alphas.append(jnp.exp2(m_old - m_new))
                ps.append(jnp.exp2(lg - jnp.concatenate([m_new] * n_lt, axis=1)).astype(BF16))
            for i in range(0, GROUP, PV_HEADS):
                rows = slice(i * tq, (i + PV_HEADS) * tq)
                alpha = jnp.concatenate(alphas[i:i + PV_HEADS], axis=0)
                pv = jnp.dot(jnp.concatenate(ps[i:i + PV_HEADS], axis=0), v_aug, preferred_element_type=F32)
                acc[g, rows, :] = acc[g, rows, :] * jnp.concatenate([alpha, alpha], axis=1) + pv
        return carry

    lax.fori_loop(0, nj, attend_chunk, 0)

    for h in range(N_HEADS):
        a = acc[h // GROUP, (h % GROUP) * tq:(h % GROUP + 1) * tq, :]
        o_ref[:, h * HEAD_DIM:(h + 1) * HEAD_DIM] = (a[:, :HEAD_DIM] / a[:, HEAD_DIM:]).astype(o_ref.dtype)


def attn_prompt(qiq, proj, col_iq, col_q, col_iw, ikt3, kt3, v3, tq=256):
    t = proj.shape[0]
    nc, _, ts = ikt3.shape
    kvw = kt3.shape[1]
    aw = N_HEADS * HEAD_DIM
    iqw = N_IDX_HEADS * IDX_DIM
    resident = dict(pipeline_mode=pl.Buffered(1))
    return pl.pallas_call(
        _attn_prompt_kernel,
        grid=(t // tq,),
        in_specs=[pl.BlockSpec((tq, iqw), lambda i: (i, col_iq)),
                  pl.BlockSpec((tq, aw), lambda i: (i, col_q)),
                  pl.BlockSpec((tq, LANES), lambda i: (i, col_iw)),
                  pl.BlockSpec((nc, IDX_DIM, ts), lambda i: (0, 0, 0), **resident),
                  pl.BlockSpec((nc, kvw, ts), lambda i: (0, 0, 0), **resident),
                  pl.BlockSpec((nc, ts, kvw), lambda i: (0, 0, 0), **resident)],
        out_specs=pl.BlockSpec((tq, aw), lambda i: (i, 0)),
        out_shape=jax.ShapeDtypeStruct((t, aw), BF16),
        scratch_shapes=[pltpu.VMEM((tq // IDX_TR, N_IDX_HEADS * IDX_TR, IDX_DIM), BF16),
                        pltpu.VMEM((tq // IDX_TR, N_IDX_HEADS * IDX_TR, LANES), F32),
                        pltpu.VMEM((tq, ts), I32),
                        pltpu.VMEM((nc, ts, tq), I32),
                        pltpu.VMEM((N_KV_HEADS, GROUP * tq, HEAD_DIM), BF16),
                        pltpu.VMEM((N_KV_HEADS, GROUP * tq, 2 * HEAD_DIM), F32),
                        pltpu.VMEM((N_KV_HEADS, GROUP * tq, LANES), F32),
                        pltpu.VMEM((tq, ts), F32),
                        pltpu.VMEM((8, tq), I32)],
        compiler_params=_cparams(("parallel",)),
        name="attn_prompt",
    )(qiq, qiq, proj, ikt3, kt3, v3)


NT_DIMS = (((1,), (1,)), ((), ()))
SAMPLE_SEARCH_BITS = 4
SAMPLE_BB = 2


def _attn_sample_kernel(pt_ref, iq_ref, q_ref, iw_ref, kn_ref, vn_ref, ikn_ref, slope_ref, *refs):
    bb, t = q_ref.shape[0], q_ref.shape[1]
    n_pages = (len(refs) - 3) // (3 * bb)
    o_ref, lg_s, cut_s = refs[3 * bb * n_pages:]
    past = n_pages * PAGE
    vk = PAGE * N_KV_HEADS
    rows = N_HEADS * t

    def page_refs(kind, b):
        start = (kind * bb + b) * n_pages
        return refs[start:start + n_pages]

    lane_t = lax.broadcasted_iota(I32, (t, PAGE), 1)
    row_t = lax.broadcasted_iota(I32, (t, PAGE), 0)

    def sequence_keys(b):
        iq_r = jnp.concatenate([iq_ref[b, :, h * IDX_DIM:(h + 1) * IDX_DIM].astype(F32)
                                for h in range(N_IDX_HEADS)], axis=0).astype(BF16)
        wb = jnp.concatenate([jnp.broadcast_to(iw_ref[b, :, h:h + 1] * IDX_SCALE, (t, LANES))
                              for h in range(N_IDX_HEADS)], axis=0)

        def page_scores(ik_page):
            d = lax.dot_general(iq_r, ik_page, NT_DIMS, preferred_element_type=F32)
            sc = jnp.zeros((t, PAGE), F32)
            for h in range(N_IDX_HEADS):
                sc = sc + wb[h * t:(h + 1) * t, :] * jnp.maximum(d[h * t:(h + 1) * t, :], 0.0)
            return sc

        ks = [_score_key(page_scores(r[0].astype(BF16)), lane_t >= 0) for r in page_refs(0, b)]
        ik_new = jnp.concatenate([ikn_ref[b], jnp.zeros((PAGE - t, IDX_DIM), F32)], axis=0)
        return ks + [_score_key(page_scores(ik_new.astype(BF16)), lane_t <= row_t)]

    seq_keys = [sequence_keys(b) for b in range(bb)]
    keys = [jnp.concatenate([seq_keys[b][p] for b in range(bb)], axis=0) for p in range(n_pages + 1)]
    lane_s = lax.broadcasted_iota(I32, (bb * t, PAGE), 1)

    def count(pred):
        cnt = jnp.zeros((bb * t, LANES), I32)
        for p, k in enumerate(keys):
            cnt = cnt + jnp.where(pred(k, p * PAGE + lane_s), 1, 0)
        return jnp.sum(cnt, axis=1, keepdims=True)

    thr_all = _kth_largest_key(lambda cand: count(lambda k, pos: k >= cand), (bb * t, LANES), SAMPLE_SEARCH_BITS)
    n_ge = count(lambda k, pos: k >= thr_all)
    cut_s[...] = jnp.full(cut_s.shape, INT_MAX, I32)

    @pl.when(jnp.max(n_ge) > INDEX_TOPK)
    def _():
        need = INDEX_TOPK - count(lambda k, pos: k > thr_all)
        cut_s[...] = _tie_cut(lambda cand: count(lambda k, pos: jnp.where(k == thr_all, pos, cand) < cand),
                              need, (bb * t, LANES), past + PAGE, SAMPLE_SEARCH_BITS)

    cut_all = cut_s[...]
    slope = slope_ref[...]

    def tiled(x, n):
        return jnp.concatenate([x] * n, axis=1)

    def head_major(ref):
        return jnp.concatenate([ref[0, pl.ds(g, PAGE, stride=N_KV_HEADS), :] for g in range(N_KV_HEADS)],
                               axis=0).astype(BF16)

    r_i = lax.broadcasted_iota(I32, (rows, vk), 0)
    c_i = lax.broadcasted_iota(I32, (rows, vk), 1)
    head_ok = (r_i // (t * GROUP)) == (c_i // PAGE)
    rel0 = (lax.broadcasted_iota(I32, (1, vk), 1) % PAGE - past).astype(F32)
    base = tiled(slope, vk // LANES) * rel0 + jnp.where(head_ok, 0.0, NEG_BIG)

    rn = lax.broadcasted_iota(I32, (rows, PAGE), 0)
    cn = lax.broadcasted_iota(I32, (rows, PAGE), 1)
    head_ok_n = ((rn // (t * GROUP)) == (cn // t)) & (cn < N_KV_HEADS * t)
    tok_n = lax.broadcasted_iota(I32, (PAGE, PAGE), 0)
    col_n = lax.broadcasted_iota(I32, (PAGE, PAGE), 1)
    expand_n = jnp.where((tok_n == col_n % t) & (col_n < N_KV_HEADS * t), 1.0, 0.0).astype(BF16)
    base_n = slope * (lax.broadcasted_iota(I32, (1, PAGE), 1) % t).astype(F32) + jnp.where(head_ok_n, 0.0, NEG_BIG)

    def attend(b):
        thr, cut = thr_all[b * t:(b + 1) * t], cut_all[b * t:(b + 1) * t]
        k_refs, v_refs = page_refs(1, b), page_refs(2, b)

        def select_bias(k, pos):
            tie = jnp.where(pos <= cut, 0.0, NEG_BIG)
            return jnp.where(k > thr, 0.0, jnp.where(k == thr, tie, NEG_BIG))

        def new_rows(ref):
            return jnp.concatenate([ref[b, :, g * HEAD_DIM:(g + 1) * HEAD_DIM] for g in range(N_KV_HEADS)]
                                   + [jnp.zeros((PAGE - N_KV_HEADS * t, HEAD_DIM), F32)], axis=0).astype(BF16)

        q_all = jnp.concatenate([q_ref[b, :, h * HEAD_DIM:(h + 1) * HEAD_DIM].astype(F32)
                                 for h in range(N_HEADS)], axis=0).astype(BF16)
        mx = jnp.full((rows, vk), -jnp.inf, F32)
        for p in range(n_pages):
            s = lax.dot_general(q_all, head_major(k_refs[p]), NT_DIMS, preferred_element_type=F32)
            sel = tiled(select_bias(seq_keys[b][p], p * PAGE + lane_t), N_KV_HEADS)
            lg = s + base + tiled(slope * float(p * PAGE), vk // LANES) \
                + jnp.concatenate([sel] * N_HEADS, axis=0)
            lg_s[b, p] = lg
            mx = jnp.maximum(mx, lg)
        s_n = lax.dot_general(q_all, new_rows(kn_ref), NT_DIMS, preferred_element_type=F32)
        sel_n = jnp.dot(select_bias(seq_keys[b][n_pages], past + lane_t).astype(BF16), expand_n,
                        preferred_element_type=F32)
        lg_n = s_n + base_n + jnp.concatenate([sel_n] * N_HEADS, axis=0)

        m_row = jnp.maximum(jnp.max(mx, axis=1, keepdims=True), jnp.max(lg_n, axis=1, keepdims=True))
        pr_n = jnp.exp2(lg_n - m_row)
        num = jnp.dot(pr_n.astype(BF16), new_rows(vn_ref), preferred_element_type=F32)
        den_n = jnp.sum(pr_n, axis=1, keepdims=True)
        den = jnp.zeros((rows, vk), F32)
        for p in range(n_pages):
            pr = jnp.exp2(lg_s[b, p] - m_row)
            den = den + pr
            num = num + jnp.dot(pr.astype(BF16), head_major(v_refs[p]), preferred_element_type=F32)
        out = num / (jnp.sum(den, axis=1, keepdims=True) + den_n)
        for h in range(N_HEADS):
            o_ref[b, :, h * HEAD_DIM:(h + 1) * HEAD_DIM] = out[h * t:(h + 1) * t, :]

    for b in range(bb):
        attend(b)


def attn_sample(page_table, qiq3, proj3, cols, cache_k, cache_v, cache_ik):
    n, t, _ = proj3.shape
    n_pages = page_table.shape[1]
    kvw = N_KV_HEADS * HEAD_DIM
    aw = N_HEADS * HEAD_DIM
    iqw = N_IDX_HEADS * IDX_DIM
    vk = PAGE * N_KV_HEADS
    slope_rows = jnp.asarray(np.repeat(np.asarray(SLOPES, np.float64) * LOG2E, t)[:, None]
                             * np.ones((1, LANES)), F32)

    bb = SAMPLE_BB
    assert n % bb == 0

    def pspec(width, col):
        return pl.BlockSpec((bb, t, width), lambda i, pt: (i, 0, col))

    def cspec(rows, width, b, p):
        return pl.BlockSpec((1, rows, width), lambda i, pt: (pt[i * bb + b, p], 0, 0))

    def page_specs(rows, width):
        return [cspec(rows, width, b, p) for b in range(bb) for p in range(n_pages)]

    grid_spec = pltpu.PrefetchScalarGridSpec(
        num_scalar_prefetch=1,
        grid=(n // bb,),
        in_specs=[pspec(iqw, cols["iq"]), pspec(aw, cols["q"]), pspec(LANES, cols["iw"]),
                  pspec(kvw, cols["k"]), pspec(kvw, cols["v"]), pspec(IDX_DIM, cols["ik"]),
                  pl.BlockSpec((N_HEADS * t, LANES), lambda i, pt: (0, 0))]
        + page_specs(PAGE, IDX_DIM) + page_specs(vk, HEAD_DIM) + page_specs(vk, HEAD_DIM),
        out_specs=pl.BlockSpec((bb, t, aw), lambda i, pt: (i, 0, 0)),
        scratch_shapes=[pltpu.VMEM((bb, n_pages, N_HEADS * t, vk), F32),
                        pltpu.VMEM((bb * t, LANES), I32)],
    )
    n_refs = bb * n_pages
    return pl.pallas_call(
        _attn_sample_kernel,
        grid_spec=grid_spec,
        out_shape=jax.ShapeDtypeStruct((n, t, aw), F32),
        compiler_params=_cparams(("parallel",)),
        name="attn_sample",
    )(page_table, qiq3, qiq3, proj3, proj3, proj3, proj3, slope_rows,
      *([cache_ik] * n_refs), *([cache_k] * n_refs), *([cache_v] * n_refs))


ATT_TS = 512


def _layer_weights(w_in, w_o, w_up, w_down, d_model):
    aw = N_HEADS * HEAD_DIM
    kvw = N_KV_HEADS * HEAD_DIM
    iqw = N_IDX_HEADS * IDX_DIM
    sizes = (aw, kvw, kvw, iqw, IDX_DIM, N_IDX_HEADS, w_in.shape[1] - (aw + 2 * kvw + iqw + IDX_DIM + N_IDX_HEADS))
    offs = [0]
    for s in sizes:
        offs.append(offs[-1] + s)
    wq, wk, wv, wiq, wik, wiw, wu = (w_in[:, offs[i]:offs[i + 1]] for i in range(7))
    wiw = jnp.pad(wiw, ((0, 0), (0, LANES - N_IDX_HEADS)))
    w_proj = jnp.concatenate([wu, wk, wv, wik, wiw, wiq, wq], axis=1).astype(BF16)
    cu = sizes[6]
    assert iqw % aw == 0 and cu % kvw == 0
    cols = {"iq": 0, "q": iqw // aw,
            "u": 0, "k": cu // kvw, "v": cu // kvw + 1, "ik": (cu + 2 * kvw) // IDX_DIM,
            "iw": (cu + 2 * kvw) // IDX_DIM + 1}
    col_scale = jnp.asarray(np.concatenate([np.ones(iqw), np.full(aw, ATT_SCALE * LOG2E)])[None, :], F32)
    return w_proj, col_scale, cols, w_o.astype(BF16), w_up.astype(BF16), w_down.astype(BF16)


def kernel(x_prompt, x_sample, cache_k, cache_v, cache_idx_k, state_conv, page_table, c_prompt, c_sample, w_ada, b_ada, g_mix, w_in, w_o, conv_w, conv_b, conv_ln_g, conv_ln_b, g_mlp, w_up, w_down, w_ada_final, b_ada_final, g_final):
    nb, seq, d = x_prompt.shape
    ns, ts_, _ = x_sample.shape
    depth = w_ada.shape[0]
    assert nb == 1 and depth == 1
    kvw = N_KV_HEADS * HEAD_DIM
    cw = conv_w.shape[2]
    k1 = CONV_K - 1

    n_c = ns + nb
    pad = (-n_c) % 16
    c_all = jnp.concatenate([c_sample, c_prompt, jnp.zeros((pad, d), F32)], axis=0)
    mods = ada_matmul(c_all, w_ada[0], b_ada[0])
    mods_f = ada_matmul(c_all, w_ada_final, b_ada_final)

    def mod_p(arr, idx):
        return arr[ns:ns + 1, idx * d:(idx + 1) * d]

    def mod_s(arr, idx):
        return jnp.repeat(arr[:ns, idx * d:(idx + 1) * d], ts_, axis=0)

    w_proj, col_scale, cols, w_o_b, w_up_b, w_down_b = _layer_weights(w_in[0], w_o[0], w_up[0], w_down[0], d)
    kcol = cols["k"] * kvw
    ikcol = cols["ik"] * IDX_DIM

    def trunk(x2, mod, attn, conv):
        x1 = out_proj_residual(attn, conv, w_o_b, x2, mod(mods, 2))
        h2 = norm_modulate(x1, g_mlp[0], mod(mods, 3), mod(mods, 4), BF16)
        m = mlp(h2, w_up_b, w_down_b)
        return residual_final_norm(x1, m, mod(mods, 5), g_final, mod(mods_f, 0), mod(mods_f, 1))

    xp = x_prompt.reshape(seq, d)
    hp = norm_modulate(xp, g_mix[0], mod_p(mods, 0), mod_p(mods, 1), BF16)
    qiq_p, proj_p = in_proj(hp, w_proj, col_scale)
    k_p = proj_p[:, kcol:kcol + kvw]
    v_p = proj_p[:, kcol + kvw:kcol + 2 * kvw]
    ik_p = proj_p[:, ikcol:ikcol + IDX_DIM]
    nc = seq // ATT_TS
    ikt3 = ik_p.astype(BF16).reshape(nc, ATT_TS, IDX_DIM).transpose(0, 2, 1)
    kt3 = k_p.astype(BF16).reshape(nc, ATT_TS, kvw).transpose(0, 2, 1)
    v3 = v_p.astype(BF16).reshape(nc, ATT_TS, kvw)
    attn_p = attn_prompt(qiq_p, proj_p, cols["iq"], cols["q"], cols["iw"], ikt3, kt3, v3)
    conv_p, cstate_p = conv_prompt(proj_p, conv_w[0], conv_b[0], conv_ln_g[0], conv_ln_b[0])
    y_p = trunk(xp, mod_p, attn_p, conv_p)

    xs = x_sample.reshape(ns * ts_, d)
    hs = norm_modulate(xs, g_mix[0], mod_s(mods, 0), mod_s(mods, 1), BF16)
    qiq_s, proj_s = in_proj(hs, w_proj, col_scale)
    proj_s3 = proj_s.reshape(ns, ts_, proj_s.shape[1])
    n_pool = cache_k.shape[1]
    attn_s = attn_sample(page_table, qiq_s.reshape(ns, ts_, qiq_s.shape[1]), proj_s3, cols,
                         cache_k[0].reshape(n_pool, PAGE * N_KV_HEADS, HEAD_DIM),
                         cache_v[0].reshape(n_pool, PAGE * N_KV_HEADS, HEAD_DIM), cache_idx_k[0])
    conv_s, cstate_s = conv_sample(proj_s3, state_conv[0], conv_w[0], conv_b[0], conv_ln_g[0], conv_ln_b[0])
    y_s = trunk(xs, mod_s, attn_s.reshape(ns * ts_, -1).astype(BF16), conv_s.reshape(ns * ts_, cw).astype(BF16))

    return (y_p.reshape(nb, seq, d),
            y_s.reshape(ns, ts_, d),
            k_p.reshape(depth, nb, seq, N_KV_HEADS, HEAD_DIM),
            v_p.reshape(depth, nb, seq, N_KV_HEADS, HEAD_DIM),
            ik_p.reshape(depth, nb, seq, IDX_DIM),
            cstate_p.reshape(depth, nb, k1, cw),
            proj_s[:, kcol:kcol + kvw].reshape(depth, ns, ts_, N_KV_HEADS, HEAD_DIM),
            proj_s[:, kcol + kvw:kcol + 2 * kvw].reshape(depth, ns, ts_, N_KV_HEADS, HEAD_DIM),
            proj_s[:, ikcol:ikcol + IDX_DIM].reshape(depth, ns, ts_, IDX_DIM),
            cstate_s.reshape(depth, ns, k1, cw))
```

```python
import functools

import jax
import jax.numpy as jnp
import numpy as np
from jax import lax
from jax.experimental import pallas as pl
from jax.experimental.pallas import tpu as pltpu

F32 = jnp.float32
BF16 = jnp.bfloat16
I32 = jnp.int32

HEAD_DIM = 128
N_HEADS = 16
N_KV_HEADS = 4
GROUP = N_HEADS // N_KV_HEADS
N_IDX_HEADS = 32
IDX_DIM = 128
INDEX_TOPK = 256
CONV_K = 31
EPS = 1e-6
PAGE = 128

LANES = 128
SUBLANES = 8
INT_MIN = -2 ** 31
NEG_BIG = -1e30
VMEM_LIMIT = 56 * 1024 * 1024

SLOPES = tuple(2.0 ** (-8.0 * (h + 1) / N_HEADS) for h in range(N_HEADS))
IDX_SCALE = (N_IDX_HEADS ** -0.5) * (IDX_DIM ** -0.5)
ATT_SCALE = HEAD_DIM ** -0.5


def _cparams(sem):
    return pltpu.CompilerParams(dimension_semantics=sem, vmem_limit_bytes=VMEM_LIMIT)


def _ada_kernel(c_ref, w_ref, b_ref, o_ref):
    c = c_ref[...]
    a = (c * jax.nn.sigmoid(c)).astype(BF16)
    o_ref[...] = jnp.dot(a, w_ref[...].astype(BF16), preferred_element_type=F32) + b_ref[...]


def ada_matmul(c, w, b, tn=512):
    m, d = c.shape
    n = w.shape[1]
    return pl.pallas_call(
        _ada_kernel,
        grid=(n // tn,),
        in_specs=[pl.BlockSpec((m, d), lambda j: (0, 0)),
                  pl.BlockSpec((d, tn), lambda j: (0, j)),
                  pl.BlockSpec((1, tn), lambda j: (0, j))],
        out_specs=pl.BlockSpec((m, tn), lambda j: (0, j)),
        out_shape=jax.ShapeDtypeStruct((m, n), F32),
        compiler_params=_cparams(("arbitrary",)),
        name="ada_matmul",
    )(c, w, b.reshape(1, n))


def _mod_spec(mod, tm, d):
    if mod.shape[0] == 1:
        return pl.BlockSpec((1, d), lambda i: (0, 0))
    return pl.BlockSpec((tm, d), lambda i: (i, 0))


def _norm_mod_kernel(x_ref, g_ref, sh_ref, sc_ref, o_ref):
    x = x_ref[...]
    y = x * lax.rsqrt(jnp.mean(x * x, axis=-1, keepdims=True) + EPS) * g_ref[...]
    o_ref[...] = (y * (1.0 + sc_ref[...]) + sh_ref[...]).astype(o_ref.dtype)


def norm_modulate(x, g, shift, scale, out_dtype, tm=256):
    m, d = x.shape
    tm = min(tm, m)
    return pl.pallas_call(
        _norm_mod_kernel,
        grid=(m // tm,),
        in_specs=[pl.BlockSpec((tm, d), lambda i: (i, 0)),
                  pl.BlockSpec((1, d), lambda i: (0, 0)),
                  _mod_spec(shift, tm, d), _mod_spec(scale, tm, d)],
        out_specs=pl.BlockSpec((tm, d), lambda i: (i, 0)),
        out_shape=jax.ShapeDtypeStruct((m, d), out_dtype),
        compiler_params=_cparams(("parallel",)),
        name="norm_modulate",
    )(x, g.reshape(1, d), shift, scale)


def _in_proj_kernel(a_ref, b_ref, s_ref, lo_ref, hi_ref):
    acc = jnp.dot(a_ref[...], b_ref[...], preferred_element_type=F32)
    hi_ref[...] = acc
    lo_ref[...] = (acc * s_ref[...]).astype(lo_ref.dtype)


def in_proj(a, b, col_scale, tm=1024, tn=768):
    m, k = a.shape
    n = b.shape[1]
    n_bf16 = col_scale.shape[1]
    tm = min(tm, m)
    n_hi = (n - n_bf16) // tn
    assert n_hi * tn == n - n_bf16 and n % tn == 0
    return pl.pallas_call(
        _in_proj_kernel,
        grid=(m // tm, n // tn),
        in_specs=[pl.BlockSpec((tm, k), lambda i, j: (i, 0)),
                  pl.BlockSpec((k, tn), lambda i, j: (0, j)),
                  pl.BlockSpec((1, tn), lambda i, j: (0, jnp.maximum(j - n_hi, 0)))],
        out_specs=[pl.BlockSpec((tm, tn), lambda i, j: (i, jnp.maximum(j - n_hi, 0))),
                   pl.BlockSpec((tm, tn), lambda i, j: (i, jnp.minimum(j, n_hi)))],
        out_shape=[jax.ShapeDtypeStruct((m, n_bf16), BF16),
                   jax.ShapeDtypeStruct((m, n - n_bf16 + tn), F32)],
        compiler_params=_cparams(("parallel", "arbitrary")),
        name="in_proj",
    )(a, b, col_scale)


def _wo_kernel(a1_ref, a2_ref, w1_ref, w2_ref, x_ref, gt_ref, o_ref):
    acc = jnp.dot(a1_ref[...].astype(BF16), w1_ref[...], preferred_element_type=F32)
    acc += jnp.dot(a2_ref[...].astype(BF16), w2_ref[...], preferred_element_type=F32)
    o_ref[...] = x_ref[...] + gt_ref[...] * acc


def out_proj_residual(attn, conv, w_o, x, gate, tm=1024, tn=512):
    m, ka = attn.shape
    kc = conv.shape[1]
    assert ka == kc
    n = w_o.shape[1]
    tm = min(tm, m)
    if gate.shape[0] == 1:
        gspec = pl.BlockSpec((1, tn), lambda i, j: (0, j))
    else:
        gspec = pl.BlockSpec((tm, tn), lambda i, j: (i, j))
    return pl.pallas_call(
        _wo_kernel,
        grid=(m // tm, n // tn),
        in_specs=[pl.BlockSpec((tm, ka), lambda i, j: (i, 0)),
                  pl.BlockSpec((tm, kc), lambda i, j: (i, 0)),
                  pl.BlockSpec((ka, tn), lambda i, j: (0, j)),
                  pl.BlockSpec((kc, tn), lambda i, j: (1, j)),
                  pl.BlockSpec((tm, tn), lambda i, j: (i, j)),
                  gspec],
        out_specs=pl.BlockSpec((tm, tn), lambda i, j: (i, j)),
        out_shape=jax.ShapeDtypeStruct((m, n), F32),
        compiler_params=_cparams(("parallel", "arbitrary")),
        name="out_proj",
    )(attn, conv, w_o, w_o, x, gate)


def _mlp_kernel(h_ref, wu_ref, wd_ref, o_ref):
    @pl.when(pl.program_id(1) == 0)
    def _():
        o_ref[...] = jnp.zeros(o_ref.shape, F32)

    up = jnp.dot(h_ref[...], wu_ref[...], preferred_element_type=F32)
    act = jnp.square(jnp.maximum(up, 0.0)).astype(BF16)
    o_ref[...] += jnp.dot(act, wd_ref[...], preferred_element_type=F32)


def mlp(h, w_up, w_down, tm=1024, tf=512):
    m, d = h.shape
    ff = w_up.shape[1]
    tm = min(tm, m)
    return pl.pallas_call(
        _mlp_kernel,
        grid=(m // tm, ff // tf),
        in_specs=[pl.BlockSpec((tm, d), lambda i, f: (i, 0), pipeline_mode=pl.Buffered(1)),
                  pl.BlockSpec((d, tf), lambda i, f: (0, f)),
                  pl.BlockSpec((tf, d), lambda i, f: (f, 0))],
        out_specs=pl.BlockSpec((tm, d), lambda i, f: (i, 0), pipeline_mode=pl.Buffered(1)),
        out_shape=jax.ShapeDtypeStruct((m, d), F32),
        compiler_params=_cparams(("parallel", "arbitrary")),
        name="mlp",
    )(h, w_up, w_down)


def _final_kernel(x_ref, m_ref, gt_ref, g_ref, sh_ref, sc_ref, o_ref):
    x = x_ref[...] + gt_ref[...] * m_ref[...]
    y = x * lax.rsqrt(jnp.mean(x * x, axis=-1, keepdims=True) + EPS) * g_ref[...]
    o_ref[...] = y * (1.0 + sc_ref[...]) + sh_ref[...]


def residual_final_norm(x, mlp_out, gate, g, shift, scale, tm=256):
    m, d = x.shape
    tm = min(tm, m)
    return pl.pallas_call(
        _final_kernel,
        grid=(m // tm,),
        in_specs=[pl.BlockSpec((tm, d), lambda i: (i, 0)),
                  pl.BlockSpec((tm, d), lambda i: (i, 0)),
                  _mod_spec(gate, tm, d),
                  pl.BlockSpec((1, d), lambda i: (0, 0)),
                  _mod_spec(shift, tm, d), _mod_spec(scale, tm, d)],
        out_specs=pl.BlockSpec((tm, d), lambda i: (i, 0)),
        out_shape=jax.ShapeDtypeStruct((m, d), F32),
        compiler_params=_cparams(("parallel",)),
        name="final_norm",
    )(x, mlp_out, gate, g.reshape(1, d), shift, scale)


def _ln_silu(y, lg, lb):
    mu = jnp.mean(y, axis=-1, keepdims=True)
    yc = y - mu
    var = jnp.mean(yc * yc, axis=-1, keepdims=True)
    yn = yc * lax.rsqrt(var + EPS) * lg + lb
    return yn * jax.nn.sigmoid(yn)


CONV_HALO = 32
CONV_RT = 64
CONV_CT = 512


def _conv_prompt_kernel(u_ref, w_ref, b_ref, lg_ref, lb_ref, o_ref, st_ref, zbuf, ybuf, zs):
    tt = u_ref.shape[0]
    c = w_ref.shape[1]
    i = pl.program_id(0)

    @pl.when(i == 0)
    def _():
        zbuf[0:CONV_HALO, :] = jnp.zeros((CONV_HALO, c), F32)

    zbuf[CONV_HALO:CONV_HALO + tt, :] = u_ref[:, 0:c] * jax.nn.sigmoid(u_ref[:, c:2 * c])
    zs_rows = zs.shape[1]
    for r in range(1, SUBLANES):
        zs[r - 1] = zbuf[r:r + zs_rows, :]
    base = CONV_HALO - (CONV_K - 1)
    for rt in range(tt // CONV_RT):
        for ct in range(c // CONV_CT):
            cs = slice(ct * CONV_CT, (ct + 1) * CONV_CT)
            acc = jnp.zeros((CONV_RT, CONV_CT), F32)
            for j in range(CONV_K):
                a, r = divmod(base + j, SUBLANES)
                r0 = a * SUBLANES + rt * CONV_RT
                window = zbuf[r0:r0 + CONV_RT, cs] if r == 0 else zs[r - 1, r0:r0 + CONV_RT, cs]
                acc = acc + w_ref[j:j + 1, cs] * window
            ybuf[rt * CONV_RT:(rt + 1) * CONV_RT, cs] = acc + b_ref[:, cs]
    o_ref[...] = _ln_silu(ybuf[...], lg_ref[...], lb_ref[...]).astype(o_ref.dtype)
    st_ref[...] = zbuf[CONV_HALO + tt - (CONV_K - 1):CONV_HALO + tt, :]
    zbuf[0:CONV_HALO, :] = zbuf[tt:tt + CONV_HALO, :]


def conv_prompt(proj, conv_w, conv_b, ln_g, ln_b, tt=256):
    t = proj.shape[0]
    c = conv_w.shape[1]
    return pl.pallas_call(
        _conv_prompt_kernel,
        grid=(t // tt,),
        in_specs=[pl.BlockSpec((tt, 2 * c), lambda i: (i, 0)),
                  pl.BlockSpec((CONV_K, c), lambda i: (0, 0)),
                  pl.BlockSpec((1, c), lambda i: (0, 0)),
                  pl.BlockSpec((1, c), lambda i: (0, 0)),
                  pl.BlockSpec((1, c), lambda i: (0, 0))],
        out_specs=[pl.BlockSpec((tt, c), lambda i: (i, 0)),
                   pl.BlockSpec((CONV_K - 1, c), lambda i: (0, 0))],
        out_shape=[jax.ShapeDtypeStruct((t, c), BF16),
                   jax.ShapeDtypeStruct((CONV_K - 1, c), F32)],
        scratch_shapes=[pltpu.VMEM((CONV_HALO + tt, c), F32), pltpu.VMEM((tt, c), F32),
                        pltpu.VMEM((SUBLANES - 1, CONV_HALO + tt - SUBLANES, c), F32)],
        compiler_params=_cparams(("arbitrary",)),
        name="conv_prompt",
    )(proj, conv_w, conv_b.reshape(1, c), ln_g.reshape(1, c), ln_b.reshape(1, c))


def _conv_sample_kernel(u_ref, s_ref, w_ref, b_ref, lg_ref, lb_ref, o_ref, ns_ref, zp):
    bb, t, _ = u_ref.shape
    c = w_ref.shape[1]
    k1 = CONV_K - 1
    for b in range(bb):
        zp[0:k1, :] = s_ref[b]
        zp[k1:k1 + t, :] = u_ref[b, :, 0:c] * jax.nn.sigmoid(u_ref[b, :, c:2 * c])
        acc = jnp.zeros((t, c), F32)
        for j in range(CONV_K):
            acc = acc + w_ref[j:j + 1, :] * zp[j:j + t, :]
        o_ref[b] = _ln_silu(acc + b_ref[...], lg_ref[...], lb_ref[...])
        ns_ref[b] = zp[t:t + k1, :]


def conv_sample(proj3, state, conv_w, conv_b, ln_g, ln_b, bb=8):
    n, t, _ = proj3.shape
    c = conv_w.shape[1]
    k1 = CONV_K - 1
    return pl.pallas_call(
        _conv_sample_kernel,
        grid=(n // bb,),
        in_specs=[pl.BlockSpec((bb, t, 2 * c), lambda i: (i, 0, 0)),
                  pl.BlockSpec((bb, k1, c), lambda i: (i, 0, 0)),
                  pl.BlockSpec((CONV_K, c), lambda i: (0, 0)),
                  pl.BlockSpec((1, c), lambda i: (0, 0)),
                  pl.BlockSpec((1, c), lambda i: (0, 0)),
                  pl.BlockSpec((1, c), lambda i: (0, 0))],
        out_specs=[pl.BlockSpec((bb, t, c), lambda i: (i, 0, 0)),
                   pl.BlockSpec((bb, k1, c), lambda i: (i, 0, 0))],
        out_shape=[jax.ShapeDtypeStruct((n, t, c), F32),
                   jax.ShapeDtypeStruct((n, k1, c), F32)],
        scratch_shapes=[pltpu.VMEM((k1 + t + 2, c), F32)],
        compiler_params=_cparams(("parallel",)),
        name="conv_sample",
    )(proj3, state, conv_w, conv_b.reshape(1, c), ln_g.reshape(1, c), ln_b.reshape(1, c))


def _score_key(score, valid):
    bits = pltpu.bitcast(score, I32)
    key = bits ^ ((bits >> 31) & 0x7FFFFFFF)
    key = jnp.maximum(key, INT_MIN + 1)
    return jnp.where(valid, key, INT_MIN)


def _digit_search(ok, start, nbits, bits_per_step):
    assert nbits % bits_per_step == 0

    def step(i, cur):
        unit = jnp.left_shift(jnp.int32(1), nbits - bits_per_step * (i + 1))
        digit = jnp.zeros(start.shape, I32)
        for c in range(1, 2 ** bits_per_step):
            digit = digit + jnp.where(ok(cur + c * unit), 1, 0)
        return cur + digit * unit

    return lax.fori_loop(0, nbits // bits_per_step, step, start)


def _kth_largest_key(count_ge, shape, bits_per_step=1):
    thr = _digit_search(lambda cand: count_ge(cand) >= INDEX_TOPK, jnp.full(shape, INT_MIN, I32), 32, bits_per_step)
    return jnp.maximum(thr, INT_MIN + 1)


def _tie_cut(count_ties_before, need, shape, n_pos, bits_per_step=1):
    nbits = -(-n_pos.bit_length() // bits_per_step) * bits_per_step
    return _digit_search(lambda cand: count_ties_before(cand) < need, jnp.zeros(shape, I32), nbits, bits_per_step)


IDX_TR = 128
IDX_TN = 256
SEL_RB = 64
LOG2E = 1.4426950408889634
INT_MAX = 2 ** 31 - 1


def _attn_prompt_kernel(iq_ref, q_ref, iw_ref, ikt_ref, kt_ref, v_ref, o_ref,
                        iq_r, wb, ktile, keys_t, q_r, acc, m_s, maskb, cut_s):
    tq = q_ref.shape[0]
    ts = ikt_ref.shape[2]
    qb = pl.program_id(0)
    q0 = qb * tq
    nj = (q0 + tq - 1) // ts + 1
    n_sub = tq // IDX_TR
    n_qt = tq // LANES

    for s in range(n_sub):
        rs = slice(s * IDX_TR, (s + 1) * IDX_TR)
        for h in range(N_IDX_HEADS):
            hs = slice(h * IDX_TR, (h + 1) * IDX_TR)
            iq_r[s, hs, :] = iq_ref[rs, h * IDX_DIM:(h + 1) * IDX_DIM]
            wb[s, hs, :] = jnp.broadcast_to(iw_ref[rs, h:h + 1] * IDX_SCALE, (IDX_TR, LANES))

    def score_chunk(j, carry):
        for s in range(n_sub):
            t_pos = q0 + s * IDX_TR + lax.broadcasted_iota(I32, (IDX_TR, LANES), 0)
            for cn in range(ts // IDX_TN):
                d = jnp.dot(iq_r[s], ikt_ref[j, :, cn * IDX_TN:(cn + 1) * IDX_TN],
                            preferred_element_type=F32)
                for half in range(IDX_TN // LANES):
                    ls = slice(half * LANES, (half + 1) * LANES)
                    sc = jnp.zeros((IDX_TR, LANES), F32)
                    for h in range(N_IDX_HEADS):
                        hs = slice(h * IDX_TR, (h + 1) * IDX_TR)
                        sc = sc + wb[s, hs, :] * jnp.maximum(d[hs, ls], 0.0)
                    c0 = cn * IDX_TN + half * LANES
                    s_pos = j * ts + c0 + lax.broadcasted_iota(I32, (IDX_TR, LANES), 1)
                    ktile[s * IDX_TR:(s + 1) * IDX_TR, c0:c0 + LANES] = _score_key(sc, s_pos <= t_pos)
        for qt in range(n_qt):
            for cn in range(ts // LANES):
                keys_t[j, cn * LANES:(cn + 1) * LANES, qt * LANES:(qt + 1) * LANES] = \
                    ktile[qt * LANES:(qt + 1) * LANES, cn * LANES:(cn + 1) * LANES].T
        return carry

    lax.fori_loop(0, nj, score_chunk, 0)

    def count_keys(pred):
        def body(j, cnt):
            for r in range(ts // SEL_RB):
                k = keys_t[j, r * SEL_RB:(r + 1) * SEL_RB, :]
                cnt = cnt + jnp.where(pred(k, j * ts + r * SEL_RB), 1, 0)
            return cnt
        cnt = lax.fori_loop(0, nj, body, jnp.zeros((SEL_RB, tq), I32))
        return jnp.sum(cnt, axis=0, keepdims=True)

    thr = _kth_largest_key(lambda cand: count_keys(lambda k, s0: k >= cand), (1, tq))
    n_ge = count_keys(lambda k, s0: k >= thr)
    cut_s[...] = jnp.full(cut_s.shape, INT_MAX, I32)

    @pl.when(jnp.max(n_ge) > INDEX_TOPK)
    def _():
        row = lax.broadcasted_iota(I32, (SEL_RB, tq), 0)
        need = INDEX_TOPK - count_keys(lambda k, s0: k > thr)

        def ties_before(cand):
            return count_keys(lambda k, s0: jnp.where(k == thr, s0 + row, cand) < cand)

        cut = _tie_cut(ties_before, need, (1, tq), keys_t.shape[0] * ts)
        cut_s[...] = jnp.broadcast_to(cut, cut_s.shape)

    for g in range(N_KV_HEADS):
        for hh in range(GROUP):
            h = g * GROUP + hh
            q_r[g, hh * tq:(hh + 1) * tq, :] = q_ref[:, h * HEAD_DIM:(h + 1) * HEAD_DIM]
    acc[...] = jnp.zeros(acc.shape, F32)
    m_s[...] = jnp.full(m_s.shape, -jnp.inf, F32)
    ones = jnp.ones((ts, HEAD_DIM), BF16)
    n_lt = ts // LANES

    def attend_chunk(j, carry):
        cut = cut_s[0:1, :]
        for r in range(n_lt):
            k = keys_t[j, r * LANES:(r + 1) * LANES, :]
            pos = j * ts + r * LANES + lax.broadcasted_iota(I32, (LANES, tq), 0)
            tie_bias = jnp.where(pos <= cut, 0.0, NEG_BIG)
            mb = jnp.where(k > thr, 0.0, jnp.where(k == thr, tie_bias, NEG_BIG))
            for qt in range(n_qt):
                maskb[qt * LANES:(qt + 1) * LANES, r * LANES:(r + 1) * LANES] = mb[:, qt * LANES:(qt + 1) * LANES].T

        rel = (j * ts - q0 + lax.broadcasted_iota(I32, (1, ts), 1)).astype(F32)
        for g in range(N_KV_HEADS):
            s_all = jnp.dot(q_r[g], kt_ref[j, g * HEAD_DIM:(g + 1) * HEAD_DIM, :],
                            preferred_element_type=F32)
            v_aug = jnp.concatenate([v_ref[j, :, g * HEAD_DIM:(g + 1) * HEAD_DIM], ones], axis=1)
            for hh in range(GROUP):
                rows = slice(hh * tq, (hh + 1) * tq)
                lg = s_all[rows] + ((SLOPES[g * GROUP + hh] * LOG2E) * rel) + maskb[...]
                m_old = m_s[g, rows, :]
                m_new = jnp.maximum(m_old, jnp.max(lg, axis=1, keepdims=True))
                m_s[g, rows, :] = m_new
                alpha = jnp.exp2(m_old - m_new)
                p = jnp.exp2(lg - jnp.concatenate([m_new] * n_lt, axis=1)).astype(BF16)
                pv = jnp.dot(p, v_aug, preferred_element_type=F32)
                acc[g, rows, :] = acc[g, rows, :] * jnp.concatenate([alpha, alpha], axis=1) + pv
        return carry

    lax.fori_loop(0, nj, attend_chunk, 0)

    for h in range(N_HEADS):
        a = acc[h // GROUP, (h % GROUP) * tq:(h % GROUP + 1) * tq, :]
        o_ref[:, h * HEAD_DIM:(h + 1) * HEAD_DIM] = (a[:, :HEAD_DIM] / a[:, HEAD_DIM:]).astype(o_ref.dtype)


def attn_prompt(qiq, proj, col_iq, col_q, col_iw, ikt3, kt3, v3, tq=256):
    t = proj.shape[0]
    nc, _, ts = ikt3.shape
    kvw = kt3.shape[1]
    aw = N_HEADS * HEAD_DIM
    iqw = N_IDX_HEADS * IDX_DIM
    resident = dict(pipeline_mode=pl.Buffered(1))
    return pl.pallas_call(
        _attn_prompt_kernel,
        grid=(t // tq,),
        in_specs=[pl.BlockSpec((tq, iqw), lambda i: (i, col_iq)),
                  pl.BlockSpec((tq, aw), lambda i: (i, col_q)),
                  pl.BlockSpec((tq, LANES), lambda i: (i, col_iw)),
                  pl.BlockSpec((nc, IDX_DIM, ts), lambda i: (0, 0, 0), **resident),
                  pl.BlockSpec((nc, kvw, ts), lambda i: (0, 0, 0), **resident),
                  pl.BlockSpec((nc, ts, kvw), lambda i: (0, 0, 0), **resident)],
        out_specs=pl.BlockSpec((tq, aw), lambda i: (i, 0)),
        out_shape=jax.ShapeDtypeStruct((t, aw), BF16),
        scratch_shapes=[pltpu.VMEM((tq // IDX_TR, N_IDX_HEADS * IDX_TR, IDX_DIM), BF16),
                        pltpu.VMEM((tq // IDX_TR, N_IDX_HEADS * IDX_TR, LANES), F32),
                        pltpu.VMEM((tq, ts), I32),
                        pltpu.VMEM((nc, ts, tq), I32),
                        pltpu.VMEM((N_KV_HEADS, GROUP * tq, HEAD_DIM), BF16),
                        pltpu.VMEM((N_KV_HEADS, GROUP * tq, 2 * HEAD_DIM), F32),
                        pltpu.VMEM((N_KV_HEADS, GROUP * tq, LANES), F32),
                        pltpu.VMEM((tq, ts), F32),
                        pltpu.VMEM((8, tq), I32)],
        compiler_params=_cparams(("parallel",)),
        name="attn_prompt",
    )(qiq, qiq, proj, ikt3, kt3, v3)


NT_DIMS = (((1,), (1,)), ((), ()))
SAMPLE_SEARCH_BITS = 4
SAMPLE_BB = 2


def _attn_sample_kernel(pt_ref, iq_ref, q_ref, iw_ref, kn_ref, vn_ref, ikn_ref, slope_ref, *refs):
    bb, t = q_ref.shape[0], q_ref.shape[1]
    n_pages = (len(refs) - 3) // (3 * bb)
    o_ref, lg_s, cut_s = refs[3 * bb * n_pages:]
    past = n_pages * PAGE
    vk = PAGE * N_KV_HEADS
    rows = N_HEADS * t

    def page_refs(kind, b):
        start = (kind * bb + b) * n_pages
        return refs[start:start + n_pages]

    lane_t = lax.broadcasted_iota(I32, (t, PAGE), 1)
    row_t = lax.broadcasted_iota(I32, (t, PAGE), 0)

    def sequence_keys(b):
        iq_r = jnp.concatenate([iq_ref[b, :, h * IDX_DIM:(h + 1) * IDX_DIM].astype(F32)
                                for h in range(N_IDX_HEADS)], axis=0).astype(BF16)
        wb = jnp.concatenate([jnp.broadcast_to(iw_ref[b, :, h:h + 1] * IDX_SCALE, (t, LANES))
                              for h in range(N_IDX_HEADS)], axis=0)

        def page_scores(ik_page):
            d = lax.dot_general(iq_r, ik_page, NT_DIMS, preferred_element_type=F32)
            sc = jnp.zeros((t, PAGE), F32)
            for h in range(N_IDX_HEADS):
                sc = sc + wb[h * t:(h + 1) * t, :] * jnp.maximum(d[h * t:(h + 1) * t, :], 0.0)
            return sc

        ks = [_score_key(page_scores(r[0].astype(BF16)), lane_t >= 0) for r in page_refs(0, b)]
        ik_new = jnp.concatenate([ikn_ref[b], jnp.zeros((PAGE - t, IDX_DIM), F32)], axis=0)
        return ks + [_score_key(page_scores(ik_new.astype(BF16)), lane_t <= row_t)]

    seq_keys = [sequence_keys(b) for b in range(bb)]
    keys = [jnp.concatenate([seq_keys[b][p] for b in range(bb)], axis=0) for p in range(n_pages + 1)]
    lane_s = lax.broadcasted_iota(I32, (bb * t, PAGE), 1)

    def count(pred):
        cnt = jnp.zeros((bb * t, LANES), I32)
        for p, k in enumerate(keys):
            cnt = cnt + jnp.where(pred(k, p * PAGE + lane_s), 1, 0)
        return jnp.sum(cnt, axis=1, keepdims=True)

    thr_all = _kth_largest_key(lambda cand: count(lambda k, pos: k >= cand), (bb * t, LANES), SAMPLE_SEARCH_BITS)
    n_ge = count(lambda k, pos: k >= thr_all)
    cut_s[...] = jnp.full(cut_s.shape, INT_MAX, I32)

    @pl.when(jnp.max(n_ge) > INDEX_TOPK)
    def _():
        need = INDEX_TOPK - count(lambda k, pos: k > thr_all)
        cut_s[...] = _tie_cut(lambda cand: count(lambda k, pos: jnp.where(k == thr_all, pos, cand) < cand),
                              need, (bb * t, LANES), past + PAGE, SAMPLE_SEARCH_BITS)

    cut_all = cut_s[...]
    slope = slope_ref[...]

    def tiled(x, n):
        return jnp.concatenate([x] * n, axis=1)

    def head_major(ref):
        return jnp.concatenate([ref[0, pl.ds(g, PAGE, stride=N_KV_HEADS), :] for g in range(N_KV_HEADS)],
                               axis=0).astype(BF16)

    r_i = lax.broadcasted_iota(I32, (rows, vk), 0)
    c_i = lax.broadcasted_iota(I32, (rows, vk), 1)
    head_ok = (r_i // (t * GROUP)) == (c_i // PAGE)
    rel0 = (lax.broadcasted_iota(I32, (1, vk), 1) % PAGE - past).astype(F32)
    base = tiled(slope, vk // LANES) * rel0 + jnp.where(head_ok, 0.0, NEG_BIG)

    rn = lax.broadcasted_iota(I32, (rows, PAGE), 0)
    cn = lax.broadcasted_iota(I32, (rows, PAGE), 1)
    head_ok_n = ((rn // (t * GROUP)) == (cn // t)) & (cn < N_KV_HEADS * t)
    tok_n = lax.broadcasted_iota(I32, (PAGE, PAGE), 0)
    col_n = lax.broadcasted_iota(I32, (PAGE, PAGE), 1)
    expand_n = jnp.where((tok_n == col_n % t) & (col_n < N_KV_HEADS * t), 1.0, 0.0).astype(BF16)
    base_n = slope * (lax.broadcasted_iota(I32, (1, PAGE), 1) % t).astype(F32) + jnp.where(head_ok_n, 0.0, NEG_BIG)

    def attend(b):
        thr, cut = thr_all[b * t:(b + 1) * t], cut_all[b * t:(b + 1) * t]
        k_refs, v_refs = page_refs(1, b), page_refs(2, b)

        def select_bias(k, pos):
            tie = jnp.where(pos <= cut, 0.0, NEG_BIG)
            return jnp.where(k > thr, 0.0, jnp.where(k == thr, tie, NEG_BIG))

        def new_rows(ref):
            return jnp.concatenate([ref[b, :, g * HEAD_DIM:(g + 1) * HEAD_DIM] for g in range(N_KV_HEADS)]
                                   + [jnp.zeros((PAGE - N_KV_HEADS * t, HEAD_DIM), F32)], axis=0).astype(BF16)

        q_all = jnp.concatenate([q_ref[b, :, h * HEAD_DIM:(h + 1) * HEAD_DIM].astype(F32)
                                 for h in range(N_HEADS)], axis=0).astype(BF16)
        mx = jnp.full((rows, vk), -jnp.inf, F32)
        for p in range(n_pages):
            s = lax.dot_general(q_all, head_major(k_refs[p]), NT_DIMS, preferred_element_type=F32)
            sel = tiled(select_bias(seq_keys[b][p], p * PAGE + lane_t), N_KV_HEADS)
            lg = s + base + tiled(slope * float(p * PAGE), vk // LANES) \
                + jnp.concatenate([sel] * N_HEADS, axis=0)
            lg_s[b, p] = lg
            mx = jnp.maximum(mx, lg)
        s_n = lax.dot_general(q_all, new_rows(kn_ref), NT_DIMS, preferred_element_type=F32)
        sel_n = jnp.dot(select_bias(seq_keys[b][n_pages], past + lane_t).astype(BF16), expand_n,
                        preferred_element_type=F32)
        lg_n = s_n + base_n + jnp.concatenate([sel_n] * N_HEADS, axis=0)

        m_row = jnp.maximum(jnp.max(mx, axis=1, keepdims=True), jnp.max(lg_n, axis=1, keepdims=True))
        pr_n = jnp.exp2(lg_n - m_row)
        num = jnp.dot(pr_n.astype(BF16), new_rows(vn_ref), preferred_element_type=F32)
        den_n = jnp.sum(pr_n, axis=1, keepdims=True)
        den = jnp.zeros((rows, vk), F32)
        for p in range(n_pages):
            pr = jnp.exp2(lg_s[b, p] - m_row)
            den = den + pr
            num = num + jnp.dot(pr.astype(BF16), head_major(v_refs[p]), preferred_element_type=F32)
        out = num / (jnp.sum(den, axis=1, keepdims=True) + den_n)
        for h in range(N_HEADS):
            o_ref[b, :, h * HEAD_DIM:(h + 1) * HEAD_DIM] = out[h * t:(h + 1) * t, :]

    for b in range(bb):
        attend(b)


def attn_sample(page_table, qiq3, proj3, cols, cache_k, cache_v, cache_ik):
    n, t, _ = proj3.shape
    n_pages = page_table.shape[1]
    kvw = N_KV_HEADS * HEAD_DIM
    aw = N_HEADS * HEAD_DIM
    iqw = N_IDX_HEADS * IDX_DIM
    vk = PAGE * N_KV_HEADS
    slope_rows = jnp.asarray(np.repeat(np.asarray(SLOPES, np.float64) * LOG2E, t)[:, None]
                             * np.ones((1, LANES)), F32)

    bb = SAMPLE_BB
    assert n % bb == 0

    def pspec(width, col):
        return pl.BlockSpec((bb, t, width), lambda i, pt: (i, 0, col))

    def cspec(rows, width, b, p):
        return pl.BlockSpec((1, rows, width), lambda i, pt: (pt[i * bb + b, p], 0, 0))

    def page_specs(rows, width):
        return [cspec(rows, width, b, p) for b in range(bb) for p in range(n_pages)]

    grid_spec = pltpu.PrefetchScalarGridSpec(
        num_scalar_prefetch=1,
        grid=(n // bb,),
        in_specs=[pspec(iqw, cols["iq"]), pspec(aw, cols["q"]), pspec(LANES, cols["iw"]),
                  pspec(kvw, cols["k"]), pspec(kvw, cols["v"]), pspec(IDX_DIM, cols["ik"]),
                  pl.BlockSpec((N_HEADS * t, LANES), lambda i, pt: (0, 0))]
        + page_specs(PAGE, IDX_DIM) + page_specs(vk, HEAD_DIM) + page_specs(vk, HEAD_DIM),
        out_specs=pl.BlockSpec((bb, t, aw), lambda i, pt: (i, 0, 0)),
        scratch_shapes=[pltpu.VMEM((bb, n_pages, N_HEADS * t, vk), F32),
                        pltpu.VMEM((bb * t, LANES), I32)],
    )
    n_refs = bb * n_pages
    return pl.pallas_call(
        _attn_sample_kernel,
        grid_spec=grid_spec,
        out_shape=jax.ShapeDtypeStruct((n, t, aw), F32),
        compiler_params=_cparams(("parallel",)),
        name="attn_sample",
    )(page_table, qiq3, qiq3, proj3, proj3, proj3, proj3, slope_rows,
      *([cache_ik] * n_refs), *([cache_k] * n_refs), *([cache_v] * n_refs))


ATT_TS = 512


def _layer_weights(w_in, w_o, w_up, w_down, d_model):
    aw = N_HEADS * HEAD_DIM
    kvw = N_KV_HEADS * HEAD_DIM
    iqw = N_IDX_HEADS * IDX_DIM
    sizes = (aw, kvw, kvw, iqw, IDX_DIM, N_IDX_HEADS, w_in.shape[1] - (aw + 2 * kvw + iqw + IDX_DIM + N_IDX_HEADS))
    offs = [0]
    for s in sizes:
        offs.append(offs[-1] + s)
    wq, wk, wv, wiq, wik, wiw, wu = (w_in[:, offs[i]:offs[i + 1]] for i in range(7))
    wiw = jnp.pad(wiw, ((0, 0), (0, LANES - N_IDX_HEADS)))
    w_proj = jnp.concatenate([wu, wk, wv, wik, wiw, wiq, wq], axis=1).astype(BF16)
    cu = sizes[6]
    assert iqw % aw == 0 and cu % kvw == 0
    cols = {"iq": 0, "q": iqw // aw,
            "u": 0, "k": cu // kvw, "v": cu // kvw + 1, "ik": (cu + 2 * kvw) // IDX_DIM,
            "iw": (cu + 2 * kvw) // IDX_DIM + 1}
    col_scale = jnp.asarray(np.concatenate([np.ones(iqw), np.full(aw, ATT_SCALE * LOG2E)])[None, :], F32)
    return w_proj, col_scale, cols, w_o.astype(BF16), w_up.astype(BF16), w_down.astype(BF16)


def kernel(x_prompt, x_sample, cache_k, cache_v, cache_idx_k, state_conv, page_table, c_prompt, c_sample, w_ada, b_ada, g_mix, w_in, w_o, conv_w, conv_b, conv_ln_g, conv_ln_b, g_mlp, w_up, w_down, w_ada_final, b_ada_final, g_final):
    nb, seq, d = x_prompt.shape
    ns, ts_, _ = x_sample.shape
    depth = w_ada.shape[0]
    assert nb == 1 and depth == 1
    kvw = N_KV_HEADS * HEAD_DIM
    cw = conv_w.shape[2]
    k1 = CONV_K - 1

    n_c = ns + nb
    pad = (-n_c) % 16
    c_all = jnp.concatenate([c_sample, c_prompt, jnp.zeros((pad, d), F32)], axis=0)
    mods = ada_matmul(c_all, w_ada[0], b_ada[0])
    mods_f = ada_matmul(c_all, w_ada_final, b_ada_final)

    def mod_p(arr, idx):
        return arr[ns:ns + 1, idx * d:(idx + 1) * d]

    def mod_s(arr, idx):
        return jnp.repeat(arr[:ns, idx * d:(idx + 1) * d], ts_, axis=0)

    w_proj, col_scale, cols, w_o_b, w_up_b, w_down_b = _layer_weights(w_in[0], w_o[0], w_up[0], w_down[0], d)
    kcol = cols["k"] * kvw
    ikcol = cols["ik"] * IDX_DIM

    def trunk(x2, mod, attn, conv):
        x1 = out_proj_residual(attn, conv, w_o_b, x2, mod(mods, 2))
        h2 = norm_modulate(x1, g_mlp[0], mod(mods, 3), mod(mods, 4), BF16)
        m = mlp(h2, w_up_b, w_down_b)
        return residual_final_norm(x1, m, mod(mods, 5), g_final, mod(mods_f, 0), mod(mods_f, 1))

    xp = x_prompt.reshape(seq, d)
    hp = norm_modulate(xp, g_mix[0], mod_p(mods, 0), mod_p(mods, 1), BF16)
    qiq_p, proj_p = in_proj(hp, w_proj, col_scale)
    k_p = proj_p[:, kcol:kcol + kvw]
    v_p = proj_p[:, kcol + kvw:kcol + 2 * kvw]
    ik_p = proj_p[:, ikcol:ikcol + IDX_DIM]
    nc = seq // ATT_TS
    ikt3 = ik_p.astype(BF16).reshape(nc, ATT_TS, IDX_DIM).transpose(0, 2, 1)
    kt3 = k_p.astype(BF16).reshape(nc, ATT_TS, kvw).transpose(0, 2, 1)
    v3 = v_p.astype(BF16).reshape(nc, ATT_TS, kvw)
    attn_p = attn_prompt(qiq_p, proj_p, cols["iq"], cols["q"], cols["iw"], ikt3, kt3, v3)
    conv_p, cstate_p = conv_prompt(proj_p, conv_w[0], conv_b[0], conv_ln_g[0], conv_ln_b[0])
    y_p = trunk(xp, mod_p, attn_p, conv_p)

    xs = x_sample.reshape(ns * ts_, d)
    hs = norm_modulate(xs, g_mix[0], mod_s(mods, 0), mod_s(mods, 1), BF16)
    qiq_s, proj_s = in_proj(hs, w_proj, col_scale)
    proj_s3 = proj_s.reshape(ns, ts_, proj_s.shape[1])
    n_pool = cache_k.shape[1]
    attn_s = attn_sample(page_table, qiq_s.reshape(ns, ts_, qiq_s.shape[1]), proj_s3, cols,
                         cache_k[0].reshape(n_pool, PAGE * N_KV_HEADS, HEAD_DIM),
                         cache_v[0].reshape(n_pool, PAGE * N_KV_HEADS, HEAD_DIM), cache_idx_k[0])
    conv_s, cstate_s = conv_sample(proj_s3, state_conv[0], conv_w[0], conv_b[0], conv_ln_g[0], conv_ln_b[0])
    y_s = trunk(xs, mod_s, attn_s.reshape(ns * ts_, -1).astype(BF16), conv_s.reshape(ns * ts_, cw).astype(BF16))

    return (y_p.reshape(nb, seq, d),
            y_s.reshape(ns, ts_, d),
            k_p.reshape(depth, nb, seq, N_KV_HEADS, HEAD_DIM),
            v_p.reshape(depth, nb, seq, N_KV_HEADS, HEAD_DIM),
            ik_p.reshape(depth, nb, seq, IDX_DIM),
            cstate_p.reshape(depth, nb, k1, cw),
            proj_s[:, kcol:kcol + kvw].reshape(depth, ns, ts_, N_KV_HEADS, HEAD_DIM),
            proj_s[:, kcol + kvw:kcol + 2 * kvw].reshape(depth, ns, ts_, N_KV_HEADS, HEAD_DIM),
            proj_s[:, ikcol:ikcol + IDX_DIM].reshape(depth, ns, ts_, IDX_DIM),
            cstate_s.reshape(depth, ns, k1, cw))
```
